```python
import math
import jax, jax.numpy as jnp
from jax import lax
import numpy as np

D_MODEL = 2048
BATCH = 2
SEQ = 4096
DEPTH = 1
DEC_BATCH = 32
DEC_SEQ = 8
PAST_LEN = 16384
PAGE_SIZE = 128

N_HEADS = 8
QK_DIM = 64
V_DIM = 2 * QK_DIM
QK_WIDTH = 2 * N_HEADS * QK_DIM
ATT_WIDTH = N_HEADS * V_DIM
GMLP_WIDTH = D_MODEL // 2
GMLP_GROUPS = 8
GROUP_DIM = GMLP_WIDTH // GMLP_GROUPS
CHUNK = 128
N_EXPERTS = 32
TOP_K = 4
D_FF = D_MODEL
SWIGLU_ALPHA = 1.702
SWIGLU_LIMIT = 7.0
PLE_DIM = 256
ROPE_THETA = 10000.0
EPS = 1e-6
Q_BLOCK = 128
NEG_INF = -1e30
IN_WIDTHS = (QK_WIDTH, QK_WIDTH, ATT_WIDTH, GMLP_WIDTH, GMLP_WIDTH, D_MODEL, D_MODEL)
IN_COLS = sum(IN_WIDTHS)
IN_SPLITS = tuple(int(c) for c in np.cumsum(IN_WIDTHS)[:-1])

kernel_name = "hybrid_diffattn_chunkgmlp_moe_decode_step"


def rms_norm(x, g):
    xf = x.astype(jnp.float32)
    y = xf * lax.rsqrt(jnp.mean(xf * xf, axis=-1, keepdims=True) + EPS)
    return (y * g.astype(jnp.float32)).astype(x.dtype)


def layer_norm(x, g, b):
    xf = x.astype(jnp.float32)
    xc = xf - jnp.mean(xf, axis=-1, keepdims=True)
    y = xc * lax.rsqrt(jnp.mean(xc * xc, axis=-1, keepdims=True) + EPS)
    return (y * g.astype(jnp.float32) + b.astype(jnp.float32)).astype(x.dtype)


def rope(x, pos):
    half = x.shape[-1] // 2
    inv = jnp.power(ROPE_THETA, -jnp.arange(half, dtype=jnp.float32) * 2.0 / x.shape[-1])
    ang = pos.astype(jnp.float32)[:, None] * inv[None, :]
    cos = jnp.cos(ang)[:, None, :]
    sin = jnp.sin(ang)[:, None, :]
    xf = x.astype(jnp.float32)
    x1, x2 = xf[..., :half], xf[..., half:]
    return jnp.concatenate([x1 * cos - x2 * sin, x2 * cos + x1 * sin], axis=-1).astype(x.dtype)


def diff_lambda(lq1, lk1, lq2, lk2, lam_init):
    f = lambda a: a.astype(jnp.float32)
    return jnp.exp(jnp.sum(f(lq1) * f(lk1))) - jnp.exp(jnp.sum(f(lq2) * f(lk2))) + lam_init


def diff_weights(s, lam, mask):
    a = jax.nn.softmax(jnp.where(mask, s, NEG_INF), axis=-1)
    return a[..., 0, :, :] - lam * a[..., 1, :, :]


def mixer_inputs(x, pos, p):
    lead = x.shape[:-1]
    z = rms_norm(x, p["g_mix"]) @ p["w_in"]
    q, k, v, u, gv, ga, gg = jnp.split(z, IN_SPLITS, axis=-1)
    q = rope(rms_norm(q.reshape(*lead, 2 * N_HEADS, QK_DIM), p["g_q"]), pos)
    k = rope(rms_norm(k.reshape(*lead, 2 * N_HEADS, QK_DIM), p["g_k"]), pos)
    v = v.reshape(*lead, N_HEADS, V_DIM)
    u = jax.nn.gelu(u)
    gv = layer_norm(jax.nn.gelu(gv), p["g_ln_v"], p["b_ln_v"])
    return q, k, v, u, gv, ga, gg


def prompt_attention(q, k, v, lam):
    B, S = q.shape[:2]
    nb = S // Q_BLOCK
    qb_all = q.reshape(B, nb, Q_BLOCK, N_HEADS, 2, QK_DIM).transpose(1, 0, 2, 3, 4, 5)
    k5 = k.reshape(B, S, N_HEADS, 2, QK_DIM)
    kpos = jnp.arange(S)
    scale = QK_DIM ** -0.5

    def block(args):
        qb, start = args
        s = jnp.einsum('bqhcd,bkhcd->bhcqk', qb, k5).astype(jnp.float32) * scale
        mask = kpos[None, :] <= (start + jnp.arange(Q_BLOCK))[:, None]
        w = diff_weights(s, lam, mask)
        return jnp.einsum('bhqk,bkhd->bqhd', w.astype(v.dtype), v)

    o = lax.map(block, (qb_all, jnp.arange(nb) * Q_BLOCK))
    return o.transpose(1, 0, 2, 3, 4).reshape(B, S, N_HEADS, V_DIM)


def sample_attention(q, k, v, cache_k, cache_v, page_table, lam):
    T = q.shape[1]
    past = page_table.shape[1] * PAGE_SIZE
    mask = jnp.arange(past + T)[None, :] <= (past + jnp.arange(T))[:, None]
    scale = QK_DIM ** -0.5

    def one(args):
        qi, ki, vi, pages = args
        kp = cache_k[pages].reshape(past, 2 * N_HEADS, QK_DIM)
        vp = cache_v[pages].reshape(past, N_HEADS, V_DIM)
        kall = jnp.concatenate([kp, ki.astype(kp.dtype)], axis=0).reshape(past + T, N_HEADS, 2, QK_DIM)
        vall = jnp.concatenate([vp, vi.astype(vp.dtype)], axis=0)
        q4 = qi.reshape(T, N_HEADS, 2, QK_DIM)
        s = jnp.einsum('qhcd,khcd->hcqk', q4, kall).astype(jnp.float32) * scale
        w = diff_weights(s, lam, mask)
        return jnp.einsum('hqk,khd->qhd', w.astype(vall.dtype), vall).astype(qi.dtype)

    return lax.map(one, (q, k, v, page_table))


def gmlp_prompt(u, gv, w_s, b_s):
    B, S, _ = gv.shape
    ws = w_s * jnp.tril(jnp.ones((CHUNK, CHUNK), w_s.dtype))
    vv = gv.reshape(B, S // CHUNK, CHUNK, GMLP_GROUPS, GROUP_DIM)
    s = jnp.einsum('gts,bcsgd->bctgd', ws, vv) + b_s.T[None, None, :, :, None]
    return u * s.reshape(B, S, GMLP_WIDTH)


def gmlp_sample(u, gv, w_s, b_s):
    DB, T, _ = gv.shape
    ws = (w_s * jnp.tril(jnp.ones((CHUNK, CHUNK), w_s.dtype)))[:, :T, :T]
    vv = gv.reshape(DB, T, GMLP_GROUPS, GROUP_DIM)
    s = jnp.einsum('gts,bsgd->btgd', ws, vv) + b_s[:, :T].T[None, :, :, None]
    return u * s.reshape(DB, T, GMLP_WIDTH)


def moe(xn, p):
    shp = xn.shape
    t = xn.reshape(-1, shp[-1])
    logits = (t @ p["w_router"] + p["b_router"]).astype(jnp.float32)
    top_v, top_i = lax.top_k(logits, TOP_K)
    wts = jax.nn.softmax(top_v, axis=-1)
    combine = jnp.sum(jax.nn.one_hot(top_i, N_EXPERTS, dtype=jnp.float32) * wts[..., None], axis=1)
    y = jnp.zeros(t.shape, jnp.float32)
    for e in range(N_EXPERTS):
        gu = t @ p["w_gate_up"][e] + p["b_gate_up"][e]
        glu = jnp.minimum(gu[:, :D_FF], SWIGLU_LIMIT)
        lin = jnp.clip(gu[:, D_FF:], -SWIGLU_LIMIT, SWIGLU_LIMIT)
        act = glu * jax.nn.sigmoid(SWIGLU_ALPHA * glu) * (lin + 1.0)
        out = act @ p["w_down"][e] + p["b_down"][e]
        y = y + combine[:, e:e + 1] * out.astype(jnp.float32)
    return y.astype(xn.dtype).reshape(shp)


def block_tail(x, att_o, gm_o, ga, gg, pe, p, lam_init):
    lead = x.shape[:-1]
    att = (rms_norm(att_o, p["g_subln"]) * (1.0 - lam_init)).reshape(*lead, ATT_WIDTH) @ p["w_att_out"]
    gm = gm_o @ p["w_gmlp_out"]
    merged = jax.nn.sigmoid(ga) * att + jax.nn.sigmoid(gg) * gm
    h = x + merged @ p["w_o"]
    h = h + moe(rms_norm(h, p["g_ffn"]), p)
    gate = jax.nn.sigmoid(rms_norm(h, p["g_ple"]) @ p["w_ple_gate"])
    return h + gate * (pe @ p["w_ple_proj"])


def setup_inputs(seed: int = 0) -> dict:
    key = jax.random.key(seed)
    ks = iter(jax.random.split(key, 40))
    f32 = jnp.float32
    nrm = lambda shape, s: jax.random.normal(next(ks), shape, f32) * s
    n_pages = PAST_LEN // PAGE_SIZE
    n_used = DEC_BATCH * n_pages
    n_pool = (n_used * 5) // 4
    page_table = jax.random.permutation(next(ks), n_pool)[:n_used].reshape(DEC_BATCH, n_pages).astype(jnp.int32)
    return {
        "x_prompt": nrm((BATCH, SEQ, D_MODEL), 1.0),
        "x_sample": nrm((DEC_BATCH, DEC_SEQ, D_MODEL), 1.0),
        "cache_k": nrm((DEPTH, n_pool, PAGE_SIZE, 2 * N_HEADS, QK_DIM), 1.0),
        "cache_v": nrm((DEPTH, n_pool, PAGE_SIZE, N_HEADS, V_DIM), 1.0),
        "page_table": page_table,
        "p_prompt": nrm((DEPTH, BATCH, SEQ, PLE_DIM), 1.0),
        "p_sample": nrm((DEPTH, DEC_BATCH, DEC_SEQ, PLE_DIM), 1.0),
        "g_mix": 1.0 + nrm((DEPTH, D_MODEL), 0.02),
        "w_in": nrm((DEPTH, D_MODEL, IN_COLS), D_MODEL ** -0.5),
        "g_q": 1.0 + nrm((DEPTH, QK_DIM), 0.02),
        "g_k": 1.0 + nrm((DEPTH, QK_DIM), 0.02),
        "lam_q1": nrm((DEPTH, QK_DIM), 0.1),
        "lam_k1": nrm((DEPTH, QK_DIM), 0.1),
        "lam_q2": nrm((DEPTH, QK_DIM), 0.1),
        "lam_k2": nrm((DEPTH, QK_DIM), 0.1),
        "g_subln": 1.0 + nrm((DEPTH, V_DIM), 0.02),
        "g_ln_v": 1.0 + nrm((DEPTH, GMLP_WIDTH), 0.02),
        "b_ln_v": nrm((DEPTH, GMLP_WIDTH), 0.02),
        "w_spatial": nrm((DEPTH, GMLP_GROUPS, CHUNK, CHUNK), CHUNK ** -0.5),
        "b_spatial": 1.0 + nrm((DEPTH, GMLP_GROUPS, CHUNK), 0.02),
        "w_att_out": nrm((DEPTH, ATT_WIDTH, D_MODEL), ATT_WIDTH ** -0.5),
        "w_gmlp_out": nrm((DEPTH, GMLP_WIDTH, D_MODEL), GMLP_WIDTH ** -0.5),
        "w_o": nrm((DEPTH, D_MODEL, D_MODEL), D_MODEL ** -0.5),
        "g_ffn": 1.0 + nrm((DEPTH, D_MODEL), 0.02),
        "w_router": nrm((DEPTH, D_MODEL, N_EXPERTS), D_MODEL ** -0.5),
        "b_router": nrm((DEPTH, N_EXPERTS), 0.01),
        "w_gate_up": nrm((DEPTH, N_EXPERTS, D_MODEL, 2 * D_FF), D_MODEL ** -0.5),
        "b_gate_up": nrm((DEPTH, N_EXPERTS, 2 * D_FF), 0.02),
        "w_down": nrm((DEPTH, N_EXPERTS, D_FF, D_MODEL), D_FF ** -0.5),
        "b_down": nrm((DEPTH, N_EXPERTS, D_MODEL), 0.02),
        "g_ple": 1.0 + nrm((DEPTH, D_MODEL), 0.02),
        "w_ple_gate": nrm((DEPTH, D_MODEL, D_MODEL), D_MODEL ** -0.5),
        "w_ple_proj": nrm((DEPTH, PLE_DIM, D_MODEL), PLE_DIM ** -0.5),
    }


def reference(x_prompt, x_sample, cache_k, cache_v, page_table, p_prompt, p_sample,
              g_mix, w_in, g_q, g_k, lam_q1, lam_k1, lam_q2, lam_k2, g_subln,
              g_ln_v, b_ln_v, w_spatial, b_spatial, w_att_out, w_gmlp_out, w_o,
              g_ffn, w_router, b_router, w_gate_up, b_gate_up, w_down, b_down,
              g_ple, w_ple_gate, w_ple_proj):
    pos_p = jnp.arange(x_prompt.shape[1])
    pos_s = page_table.shape[1] * PAGE_SIZE + jnp.arange(x_sample.shape[1])
    hp, hs = x_prompt, x_sample
    kp_rows, vp_rows, ks_rows, vs_rows, gs_rows = [], [], [], [], []
    for i in range(DEPTH):
        p = dict(g_mix=g_mix[i], w_in=w_in[i], g_q=g_q[i], g_k=g_k[i], g_subln=g_subln[i],
                 g_ln_v=g_ln_v[i], b_ln_v=b_ln_v[i], w_att_out=w_att_out[i],
                 w_gmlp_out=w_gmlp_out[i], w_o=w_o[i], g_ffn=g_ffn[i], w_router=w_router[i],
                 b_router=b_router[i], w_gate_up=w_gate_up[i], b_gate_up=b_gate_up[i],
                 w_down=w_down[i], b_down=b_down[i], g_ple=g_ple[i], w_ple_gate=w_ple_gate[i],
                 w_ple_proj=w_ple_proj[i])
        lam_init = 0.8 - 0.6 * math.exp(-0.3 * i)
        lam = diff_lambda(lam_q1[i], lam_k1[i], lam_q2[i], lam_k2[i], lam_init)
        q, k, v, u, gv, ga, gg = mixer_inputs(hp, pos_p, p)
        att = prompt_attention(q, k, v, lam)
        gm = gmlp_prompt(u, gv, w_spatial[i], b_spatial[i])
        hp = block_tail(hp, att, gm, ga, gg, p_prompt[i], p, lam_init)
        kp_rows.append(k)
        vp_rows.append(v)
        q, k, v, u, gv, ga, gg = mixer_inputs(hs, pos_s, p)
        att = sample_attention(q, k, v, cache_k[i], cache_v[i], page_table, lam)
        gm = gmlp_sample(u, gv, w_spatial[i], b_spatial[i])
        hs = block_tail(hs, att, gm, ga, gg, p_sample[i], p, lam_init)
        ks_rows.append(k)
        vs_rows.append(v)
        gs_rows.append(gv)
    return (hp, hs, jnp.stack(kp_rows), jnp.stack(vp_rows), jnp.stack(ks_rows), jnp.stack(vs_rows), jnp.stack(gs_rows))
```

```python
import functools
import math

import jax
import jax.numpy as jnp
from jax import lax
from jax.experimental import pallas as pl
from jax.experimental.pallas import tpu as pltpu

EPS = 1e-6
NEG_INF = -1e30
MASKED = -3.0e38
ROPE_THETA = 10000.0
SWIGLU_ALPHA = 1.702
SWIGLU_LIMIT = 7.0
TOP_K = 4
LANES = 128
SUBLANES = 8
VMEM_LIMIT_BYTES = 56 * 1024 * 1024
BF16 = jnp.bfloat16
F32 = jnp.float32


def _pick_tile(n, candidates):
    for c in candidates:
        if n % c == 0:
            return c
    raise ValueError(f"no tile in {candidates} divides {n}")


def _params(n_axes):
    return pltpu.CompilerParams(
        dimension_semantics=("arbitrary",) * n_axes,
        vmem_limit_bytes=VMEM_LIMIT_BYTES)


def _gelu(x):
    c = math.sqrt(2.0 / math.pi)
    return x * (0.5 * (1.0 + jnp.tanh(c * (x + 0.044715 * (x * x * x)))))


def _sigmoid(x):
    return 1.0 / (1.0 + jnp.exp(-x))


def _rmsnorm_kernel(x_ref, g_ref, o_ref):
    x = x_ref[...]
    y = x * lax.rsqrt(jnp.mean(x * x, axis=-1, keepdims=True) + EPS) * g_ref[...]
    o_ref[...] = y.astype(o_ref.dtype)


def _rmsnorm(x, g, tm):
    m, d = x.shape
    return pl.pallas_call(
        _rmsnorm_kernel,
        grid=(m // tm,),
        in_specs=[pl.BlockSpec((tm, d), lambda i: (i, 0)),
                  pl.BlockSpec((1, d), lambda i: (0, 0))],
        out_specs=pl.BlockSpec((tm, d), lambda i: (i, 0)),
        out_shape=jax.ShapeDtypeStruct((m, d), BF16),
        compiler_params=_params(1),
    )(x, g.reshape(1, d))


def _mm_kernel(*refs, epilogue, n_extra):
    x_ref, w_ref = refs[0], refs[1]
    extra = refs[2:2 + n_extra]
    outs = refs[2 + n_extra:]
    acc = jnp.dot(x_ref[...], w_ref[...], preferred_element_type=F32)
    epilogue(acc, extra, outs)


def _mm(x, w, epilogue, *, tm, tn, n_col_blocks, w_col0=0, extras=(), out_dtypes):
    m, k = x.shape
    nj, ni = n_col_blocks, m // tm
    in_specs = [pl.BlockSpec((tm, k), lambda j, i: (i, 0)),
                pl.BlockSpec((k, tn), lambda j, i: (0, w_col0 + j))]
    args = [x, w]
    for arr, kind in extras:
        if kind == "tile":
            spec = pl.BlockSpec((tm, tn), lambda j, i: (i, j))
        elif kind == "tile2":
            spec = pl.BlockSpec((tm, tn), lambda j, i: (i, nj + j))
        elif kind == "row":
            spec = pl.BlockSpec((tm, arr.shape[1]), lambda j, i: (i, 0))
        elif kind == "col":
            spec = pl.BlockSpec((1, tn), lambda j, i: (0, j))
        elif kind == "colmat":
            spec = pl.BlockSpec((arr.shape[0], tn), lambda j, i: (0, j))
        elif kind == "full":
            spec = pl.BlockSpec(arr.shape, lambda j, i, nd=arr.ndim: (0,) * nd)
        else:
            raise ValueError(kind)
        in_specs.append(spec)
        args.append(arr)
    out_specs = [pl.BlockSpec((tm, tn), lambda j, i: (i, j)) for _ in out_dtypes]
    out_shape = [jax.ShapeDtypeStruct((m, nj * tn), dt) for dt in out_dtypes]
    res = pl.pallas_call(
        functools.partial(_mm_kernel, epilogue=epilogue, n_extra=len(extras)),
        grid=(nj, ni),
        in_specs=in_specs,
        out_specs=out_specs,
        out_shape=out_shape,
        compiler_params=_params(2),
    )(*args)
    return res


def _ep_qk(acc, extra, outs, *, dk):
    g_ref, scale_ref, cos_ref, sin_ref, blk_ref = extra
    o32, o16 = outs
    cos = cos_ref[...]
    sin = sin_ref[...]
    blk = blk_ref[...]
    half = dk // 2
    lane = lax.broadcasted_iota(jnp.int32, (acc.shape[0], LANES), 1)
    upper = (lane & half) != 0
    for cb in range(acc.shape[1] // LANES):
        sl = slice(cb * LANES, (cb + 1) * LANES)
        z = acc[:, sl]
        ss = jnp.dot((z * z).astype(BF16), blk, preferred_element_type=F32)
        y = z * lax.rsqrt(ss * (1.0 / dk) + EPS) * g_ref[:, sl]
        partner = jnp.where(upper, pltpu.roll(y, half, 1), pltpu.roll(y, LANES - half, 1))
        r = y * cos + partner * sin
        o32[:, sl] = r
        o16[:, sl] = (r * scale_ref[:, sl]).astype(BF16)


def _ep_copy2(acc, extra, outs):
    outs[0][...] = acc
    outs[1][...] = acc.astype(BF16)


def _ep_gelu(acc, extra, outs):
    outs[0][...] = _gelu(acc).astype(outs[0].dtype)


def _ep_gelu_ln(acc, extra, outs):
    g_ref, b_ref = extra
    y = _gelu(acc)
    yc = y - jnp.mean(y, axis=-1, keepdims=True)
    z = yc * lax.rsqrt(jnp.mean(yc * yc, axis=-1, keepdims=True) + EPS)
    outs[0][...] = z * g_ref[...] + b_ref[...]


def _ep_sigmoid(acc, extra, outs):
    outs[0][...] = _sigmoid(acc).astype(outs[0].dtype)


def _ep_residual(acc, extra, outs):
    outs[0][...] = extra[0][...] + acc


def _ep_ple(acc, extra, outs):
    h_ref, pe_ref, wp_ref = extra
    proj = jnp.dot(pe_ref[...], wp_ref[...], preferred_element_type=F32)
    outs[0][...] = h_ref[...] + _sigmoid(acc) * proj


def _diff_lambda(lam_ref, lam_init):
    v = lam_ref[...]
    a = jnp.sum(v[0:1, :] * v[1:2, :], axis=-1, keepdims=True)
    b = jnp.sum(v[2:3, :] * v[3:4, :], axis=-1, keepdims=True)
    return jnp.exp(a) - jnp.exp(b) + lam_init


def _subln(o, g, lam_init):
    y = o * lax.rsqrt(jnp.mean(o * o, axis=-1, keepdims=True) + EPS) * g
    return y * (1.0 - lam_init)


def _prompt_attn_kernel(q_ref, k_ref, v_ref, lam_ref, g_ref, o_ref,
                        m_sc, l_sc, acc_sc, *, tq, dk, lam_init):
    qi = pl.program_id(2)
    q = q_ref[...]
    lane = lax.broadcasted_iota(jnp.int32, q.shape, 1)
    zero = jnp.zeros_like(q)
    q2 = jnp.concatenate([jnp.where(lane < dk, q, zero),
                          jnp.where(lane >= dk, q, zero)], axis=0)
    m_sc[...] = jnp.full(m_sc.shape, NEG_INF, F32)
    l_sc[...] = jnp.zeros(l_sc.shape, F32)
    acc_sc[...] = jnp.zeros(acc_sc.shape, F32)

    def step(j, masked):
        start = pl.multiple_of(j * tq, tq)
        kj = k_ref[pl.ds(start, tq), :]
        vj = v_ref[pl.ds(start, tq), :]
        s = lax.dot_general(q2, kj, (((1,), (1,)), ((), ())),
                            preferred_element_type=F32)
        if masked:
            row = lax.broadcasted_iota(jnp.int32, (2 * tq, tq), 0)
            col = lax.broadcasted_iota(jnp.int32, (2 * tq, tq), 1)
            row = jnp.where(row >= tq, row - tq, row)
            s = jnp.where(col <= row, s, NEG_INF)
        m_old = m_sc[...]
        m_new = jnp.maximum(m_old, jnp.max(s, axis=-1, keepdims=True))
        alpha = jnp.exp(m_old - m_new)
        p = jnp.exp(s - m_new)
        l_sc[...] = alpha * l_sc[...] + jnp.sum(p, axis=-1, keepdims=True)
        acc_sc[...] = alpha * acc_sc[...] + jnp.dot(
            p.astype(BF16), vj, preferred_element_type=F32)
        m_sc[...] = m_new

    def body(j, carry):
        step(j, False)
        return carry

    lax.fori_loop(0, qi, body, 0)
    step(qi, True)

    lam = _diff_lambda(lam_ref, lam_init)
    o = acc_sc[...] / l_sc[...]
    od = o[:tq] - lam * o[tq:]
    o_ref[...] = _subln(od, g_ref[...], lam_init).astype(o_ref.dtype)


def _prompt_attention(qk16, v16, lam_vec, g_subln, *, b, s, h, dk, lam_init):
    dv = 2 * dk
    assert dv == LANES
    tq = _pick_tile(s, (256, 128))
    nq = s // tq
    return pl.pallas_call(
        functools.partial(_prompt_attn_kernel, tq=tq, dk=dk, lam_init=lam_init),
        grid=(b, h, nq),
        in_specs=[pl.BlockSpec((tq, LANES), lambda bi, hi, qi: (bi * nq + qi, hi)),
                  pl.BlockSpec((s, LANES), lambda bi, hi, qi: (bi, h + hi)),
                  pl.BlockSpec((s, LANES), lambda bi, hi, qi: (bi, hi)),
                  pl.BlockSpec((SUBLANES, LANES), lambda bi, hi, qi: (0, 0)),
                  pl.BlockSpec((1, LANES), lambda bi, hi, qi: (0, 0))],
        out_specs=pl.BlockSpec((tq, LANES), lambda bi, hi, qi: (bi * nq + qi, hi)),
        out_shape=jax.ShapeDtypeStruct((b * s, h * dv), BF16),
        scratch_shapes=[pltpu.VMEM((2 * tq, 1), F32),
                        pltpu.VMEM((2 * tq, 1), F32),
                        pltpu.VMEM((2 * tq, LANES), F32)],
        compiler_params=_params(3),
    )(qk16, qk16, v16, lam_vec, g_subln.reshape(1, dv))


def _sample_attn_kernel(pt_ref, qbd_ref, knew_ref, vnew_ref, lam_ref, g_ref, *rest,
                        pp, page, n_heads, t_new, lam_init):
    kpages = rest[:pp]
    vpages = rest[pp:2 * pp]
    o_ref = rest[2 * pp]
    kb_sc, vb_sc, m_sc, l_sc, acc_sc = rest[2 * pp + 1:]
    step = pl.program_id(1)
    n_steps = pl.num_programs(1)
    dv = LANES

    @pl.when(step == 0)
    def _():
        m_sc[...] = jnp.full(m_sc.shape, NEG_INF, F32)
        l_sc[...] = jnp.zeros(l_sc.shape, F32)
        acc_sc[...] = jnp.zeros(acc_sc.shape, F32)

    qbd = qbd_ref[...]

    def update(s, vb):
        m_old = m_sc[...]
        m_new = jnp.maximum(m_old, jnp.max(s, axis=-1, keepdims=True))
        alpha = jnp.exp(m_old - m_new)
        p = jnp.exp(s - m_new)
        l_sc[...] = alpha * l_sc[...] + jnp.sum(p, axis=-1, keepdims=True)
        acc_sc[...] = alpha * acc_sc[...] + jnp.dot(
            p.astype(BF16), vb, preferred_element_type=F32)
        m_sc[...] = m_new

    for p_i in range(pp):
        kb_sc[pl.ds(p_i * page, page), :] = kpages[p_i][...].astype(BF16)
        vb_sc[pl.ds(p_i * page, page), :] = vpages[p_i][...].astype(BF16)
    s = lax.dot_general(qbd, kb_sc[...], (((1,), (1,)), ((), ())),
                        preferred_element_type=F32)
    update(s, vb_sc[...])

    @pl.when(step == n_steps - 1)
    def _():
        rows = qbd.shape[0]
        width = knew_ref.shape[1]
        pad = jnp.zeros((LANES - t_new, width), F32)
        kn = jnp.concatenate([knew_ref[...], pad], axis=0).astype(BF16)
        vn = jnp.concatenate([vnew_ref[...], pad], axis=0).astype(BF16)
        sn = lax.dot_general(qbd, kn, (((1,), (1,)), ((), ())),
                             preferred_element_type=F32)
        row = lax.broadcasted_iota(jnp.int32, (rows, LANES), 0)
        col = lax.broadcasted_iota(jnp.int32, (rows, LANES), 1)
        keep = col <= (row & (t_new - 1))
        update(jnp.where(keep, sn, NEG_INF), vn)

        lam = _diff_lambda(lam_ref, lam_init)
        o = acc_sc[...] / l_sc[...]
        for hh in range(n_heads):
            r1 = (2 * hh) * t_new
            r2 = (2 * hh + 1) * t_new
            cs = slice(hh * dv, (hh + 1) * dv)
            od = o[r1:r1 + t_new, cs] - lam * o[r2:r2 + t_new, cs]
            o_ref[:, cs] = _subln(od, g_ref[...], lam_init)


def _sample_attention(qbd, k_new, v_new, cache_k, cache_v, page_table, lam_vec, g_subln,
                      *, new_row0, n_heads, t_new, lam_init):
    db, rows, width = qbd.shape
    n_pool, page = cache_k.shape[0], cache_k.shape[1]
    n_pages = page_table.shape[1]
    pp = _pick_tile(n_pages, (8, 4, 2, 1))
    n_steps = n_pages // pp
    dv = LANES

    def page_spec(p_i):
        return pl.BlockSpec((None, page, width),
                            lambda b, st, pt: (pt[b, st * pp + p_i], 0, 0))

    in_specs = [pl.BlockSpec((None, rows, width), lambda b, st, pt: (b, 0, 0)),
                pl.BlockSpec((t_new, width), lambda b, st, pt: (new_row0 + b, 1)),
                pl.BlockSpec((t_new, width), lambda b, st, pt: (new_row0 + b, 0)),
                pl.BlockSpec((SUBLANES, LANES), lambda b, st, pt: (0, 0)),
                pl.BlockSpec((1, dv), lambda b, st, pt: (0, 0))]
    in_specs += [page_spec(p_i) for p_i in range(pp)]
    in_specs += [page_spec(p_i) for p_i in range(pp)]
    grid_spec = pltpu.PrefetchScalarGridSpec(
        num_scalar_prefetch=1,
        grid=(db, n_steps),
        in_specs=in_specs,
        out_specs=pl.BlockSpec((t_new, n_heads * dv), lambda b, st, pt: (b, 0)),
        scratch_shapes=[pltpu.VMEM((pp * page, width), BF16),
                        pltpu.VMEM((pp * page, n_heads * dv), BF16),
                        pltpu.VMEM((rows, 1), F32),
                        pltpu.VMEM((rows, 1), F32),
                        pltpu.VMEM((rows, n_heads * dv), F32)])
    return pl.pallas_call(
        functools.partial(_sample_attn_kernel, pp=pp, page=page, n_heads=n_heads,
                          t_new=t_new, lam_init=lam_init),
        grid_spec=grid_spec,
        out_shape=jax.ShapeDtypeStruct((db * t_new, n_heads * dv), F32),
        compiler_params=_params(2),
    )(page_table, qbd, k_new, v_new, lam_vec, g_subln.reshape(1, dv),
      *([cache_k] * pp), *([cache_v] * pp))


def _gmlp_kernel(u_ref, gv_ref, w_ref, b_ref, o_ref, *, groups):
    for g in range(groups):
        sl = slice(g * LANES, (g + 1) * LANES)
        s = jnp.dot(w_ref[g], gv_ref[:, sl].astype(BF16), preferred_element_type=F32)
        s = s + b_ref[g]
        o_ref[:, sl] = (u_ref[:, sl].astype(F32) * s).astype(o_ref.dtype)


def _gmlp(u16, gv, wmix, bmix, *, n_prompt_chunks):
    t, width = gv.shape
    groups, chunk = wmix.shape[1], wmix.shape[2]
    assert chunk == LANES and width == groups * LANES
    sel = lambda c: (jnp.where(c < n_prompt_chunks, 0, 1), 0, 0, 0)
    return pl.pallas_call(
        functools.partial(_gmlp_kernel, groups=groups),
        grid=(t // chunk,),
        in_specs=[pl.BlockSpec((chunk, width), lambda c: (c, 0)),
                  pl.BlockSpec((chunk, width), lambda c: (c, 0)),
                  pl.BlockSpec((None, groups, chunk, chunk), sel),
                  pl.BlockSpec((None, groups, chunk, LANES), sel)],
        out_specs=pl.BlockSpec((chunk, width), lambda c: (c, 0)),
        out_shape=jax.ShapeDtypeStruct((t, width), BF16),
        compiler_params=_params(1),
    )(u16, gv, wmix, bmix)


def _merge_kernel(att_ref, gm_ref, wa_ref, wg_ref, ga_ref, gg_ref, o_ref):
    a = jnp.dot(att_ref[...], wa_ref[...], preferred_element_type=F32)
    g = jnp.dot(gm_ref[...], wg_ref[...], preferred_element_type=F32)
    o_ref[...] = (ga_ref[...].astype(F32) * a + gg_ref[...].astype(F32) * g).astype(o_ref.dtype)


def _merge(att16, gm16, wa, wg, gates16, *, tm, tn):
    t, ka = att16.shape
    kg = gm16.shape[1]
    d = wa.shape[1]
    nj = d // tn
    return pl.pallas_call(
        _merge_kernel,
        grid=(nj, t // tm),
        in_specs=[pl.BlockSpec((tm, ka), lambda j, i: (i, 0)),
                  pl.BlockSpec((tm, kg), lambda j, i: (i, 0)),
                  pl.BlockSpec((ka, tn), lambda j, i: (0, j)),
                  pl.BlockSpec((kg, tn), lambda j, i: (0, j)),
                  pl.BlockSpec((tm, tn), lambda j, i: (i, j)),
                  pl.BlockSpec((tm, tn), lambda j, i: (i, nj + j))],
        out_specs=pl.BlockSpec((tm, tn), lambda j, i: (i, j)),
        out_shape=jax.ShapeDtypeStruct((t, d), BF16),
        compiler_params=_params(2),
    )(att16, gm16, wa, wg, gates16, gates16)


def _router_kernel(h_ref, g_ref, wh_ref, wl_ref, b_ref, pk_ref, ti_ref, tw_ref,
                   *, n_experts):
    x = h_ref[...]
    xn = x * lax.rsqrt(jnp.mean(x * x, axis=-1, keepdims=True) + EPS) * g_ref[...]
    tm, d = xn.shape
    xh = xn.astype(BF16)
    xl = (xn - xh.astype(F32)).astype(BF16)
    wh = wh_ref[...]
    logits = (jnp.dot(xh, wh, preferred_element_type=F32)
              + jnp.dot(xl, wh, preferred_element_type=F32)
              + jnp.dot(xh, wl_ref[...], preferred_element_type=F32)) + b_ref[...]
    lane = lax.broadcasted_iota(jnp.int32, logits.shape, 1)
    lane_f = lane.astype(F32)
    cur = jnp.where(lane < n_experts, logits, MASKED)
    vals, idxs = [], []
    for _ in range(TOP_K):
        m = jnp.max(cur, axis=-1, keepdims=True)
        idx = jnp.min(jnp.where(cur == m, lane_f, float(LANES)), axis=-1, keepdims=True)
        vals.append(m)
        idxs.append(idx)
        cur = jnp.where(lane_f == idx, MASKED, cur)
    es = [jnp.exp(v - vals[0]) for v in vals]
    tot = es[0]
    for e in es[1:]:
        tot = tot + e
    ti = jnp.zeros(logits.shape, F32)
    tw = jnp.zeros(logits.shape, F32)
    for kk in range(TOP_K):
        ti = jnp.where(lane == kk, idxs[kk], ti)
        tw = jnp.where(lane == kk, es[kk] / tot, tw)
    ti_ref[...] = ti.astype(jnp.int32)
    tw_ref[...] = tw
    half = d // 2
    xr = xh.astype(F32)
    for s in range(half // LANES):
        hi = pltpu.bitcast(xr[:, s * LANES:(s + 1) * LANES], jnp.uint32)
        lo = pltpu.bitcast(xr[:, half + s * LANES:half + (s + 1) * LANES], jnp.uint32)
        word = (hi & jnp.uint32(0xFFFF0000)) | (lo >> 16)
        pk_ref[pl.ds(s, tm, stride=half // LANES), :] = word


def _router(h, g_ffn, wr_hi, wr_lo, b_pad, *, n_experts, tm):
    t, d = h.shape
    spt = d // 2 // LANES
    return pl.pallas_call(
        functools.partial(_router_kernel, n_experts=n_experts),
        grid=(t // tm,),
        in_specs=[pl.BlockSpec((tm, d), lambda i: (i, 0)),
                  pl.BlockSpec((1, d), lambda i: (0, 0)),
                  pl.BlockSpec((d, LANES), lambda i: (0, 0)),
                  pl.BlockSpec((d, LANES), lambda i: (0, 0)),
                  pl.BlockSpec((1, LANES), lambda i: (0, 0))],
        out_specs=[pl.BlockSpec((tm * spt, LANES), lambda i: (i, 0)),
                   pl.BlockSpec((tm, LANES), lambda i: (i, 0)),
                   pl.BlockSpec((tm, LANES), lambda i: (i, 0))],
        out_shape=[jax.ShapeDtypeStruct((t * spt, LANES), jnp.uint32),
                   jax.ShapeDtypeStruct((t, LANES), jnp.int32),
                   jax.ShapeDtypeStruct((t, LANES), F32)],
        compiler_params=_params(1),
    )(h, g_ffn.reshape(1, d), wr_hi, wr_lo, b_pad)


def _gather_kernel(idx_ref, nv_ref, src_ref, o_ref, sem, *, tm, spt):
    t = pl.program_id(0)
    valid = t < nv_ref[0]

    def copy(r):
        tok = idx_ref[t * tm + r]
        return pltpu.make_async_copy(
            src_ref.at[pl.ds(pl.multiple_of(tok * spt, spt), spt), :],
            o_ref.at[pl.ds(pl.multiple_of(r * spt, spt), spt), :],
            sem)

    @pl.when(valid)
    def _():
        def start(r, c):
            copy(r).start()
            return c
        lax.fori_loop(0, tm, start, 0)

        def wait(r, c):
            copy(r).wait()
            return c
        lax.fori_loop(0, tm, wait, 0)

    @pl.when(jnp.logical_not(valid))
    def _():
        o_ref[...] = jnp.zeros(o_ref.shape, o_ref.dtype)


def _gather_rows(src, idx, n_valid_tiles, *, n_tiles, tm, spt):
    grid_spec = pltpu.PrefetchScalarGridSpec(
        num_scalar_prefetch=2,
        grid=(n_tiles,),
        in_specs=[pl.BlockSpec(memory_space=pl.ANY)],
        out_specs=pl.BlockSpec((tm * spt, LANES), lambda t, idx, nv: (t, 0)),
        scratch_shapes=[pltpu.SemaphoreType.DMA(())])
    return pl.pallas_call(
        functools.partial(_gather_kernel, tm=tm, spt=spt),
        grid_spec=grid_spec,
        out_shape=jax.ShapeDtypeStruct((n_tiles * tm * spt, LANES), src.dtype),
        compiler_params=_params(1),
    )(idx, n_valid_tiles, src)


def _unpack_rows(pk_ref, tm, spt):
    his, los = [], []
    for s in range(spt):
        w = pk_ref[pl.ds(s, tm, stride=spt), :]
        his.append(pltpu.bitcast(w & jnp.uint32(0xFFFF0000), F32))
        los.append(pltpu.bitcast(w << 16, F32))
    return jnp.concatenate(his + los, axis=1).astype(BF16)


def _new_group(te_ref, t):
    prev = te_ref[jnp.maximum(t - 1, 0)]
    return jnp.logical_or(t == 0, te_ref[t] != prev)


def _moe_up_kernel(te_ref, nv_ref, xs_ref, wg_ref, wl_ref, bg_ref, bl_ref, o_ref,
                   wg_sc, wl_sc, *, tm, spt):
    t = pl.program_id(1)
    valid = t < nv_ref[0]

    @pl.when(jnp.logical_and(valid, _new_group(te_ref, t)))
    def _():
        wg_sc[...] = wg_ref[...].astype(BF16)
        wl_sc[...] = wl_ref[...].astype(BF16)

    @pl.when(valid)
    def _():
        x = _unpack_rows(xs_ref, tm, spt)
        gl = jnp.dot(x, wg_sc[...], preferred_element_type=F32) + bg_ref[...]
        ln = jnp.dot(x, wl_sc[...], preferred_element_type=F32) + bl_ref[...]
        glu = jnp.minimum(gl, SWIGLU_LIMIT)
        lin = jnp.clip(ln, -SWIGLU_LIMIT, SWIGLU_LIMIT)
        act = glu * _sigmoid(SWIGLU_ALPHA * glu) * (lin + 1.0)
        o_ref[...] = act.astype(o_ref.dtype)

    @pl.when(jnp.logical_not(valid))
    def _():
        o_ref[...] = jnp.zeros(o_ref.shape, o_ref.dtype)


def _moe_up(xs, w_gate_up, b_gate_up, tile_expert, n_valid, *, n_tiles, tm, spt, fc):
    e, d, f2 = w_gate_up.shape
    f = f2 // 2
    nc = f // fc
    last = lambda t, nv: jnp.minimum(t, nv[0] - 1)
    grid_spec = pltpu.PrefetchScalarGridSpec(
        num_scalar_prefetch=2,
        grid=(nc, n_tiles),
        in_specs=[pl.BlockSpec((tm * spt, LANES), lambda c, t, te, nv: (last(t, nv), 0)),
                  pl.BlockSpec((None, d, fc), lambda c, t, te, nv: (te[t], 0, c)),
                  pl.BlockSpec((None, d, fc), lambda c, t, te, nv: (te[t], 0, nc + c)),
                  pl.BlockSpec((None, 1, fc), lambda c, t, te, nv: (te[t], 0, c)),
                  pl.BlockSpec((None, 1, fc), lambda c, t, te, nv: (te[t], 0, nc + c))],
        out_specs=pl.BlockSpec((tm, fc), lambda c, t, te, nv: (t, c)),
        scratch_shapes=[pltpu.VMEM((d, fc), BF16), pltpu.VMEM((d, fc), BF16)])
    return pl.pallas_call(
        functools.partial(_moe_up_kernel, tm=tm, spt=spt),
        grid_spec=grid_spec,
        out_shape=jax.ShapeDtypeStruct((n_tiles * tm, f), BF16),
        compiler_params=_params(2),
    )(tile_expert, n_valid, xs, w_gate_up, w_gate_up,
      b_gate_up.reshape(e, 1, f2), b_gate_up.reshape(e, 1, f2))


def _moe_down_kernel(te_ref, nv_ref, a_ref, w_ref, b_ref, o_ref, w_sc, *, tm, spo):
    t = pl.program_id(1)
    valid = t < nv_ref[0]

    @pl.when(jnp.logical_and(valid, _new_group(te_ref, t)))
    def _():
        w_sc[...] = w_ref[...].astype(BF16)

    @pl.when(valid)
    def _():
        y = jnp.dot(a_ref[...], w_sc[...], preferred_element_type=F32) + b_ref[...]
        for s in range(y.shape[1] // LANES):
            o_ref[pl.ds(s, tm, stride=spo), :] = y[:, s * LANES:(s + 1) * LANES]

    @pl.when(jnp.logical_not(valid))
    def _():
        o_ref[...] = jnp.zeros(o_ref.shape, o_ref.dtype)


def _moe_down(act, w_down, b_down, tile_expert, n_valid, *, n_tiles, tm, tn):
    e, f, d = w_down.shape
    nn = d // tn
    spo = tn // LANES
    last = lambda t, nv: jnp.minimum(t, nv[0] - 1)
    grid_spec = pltpu.PrefetchScalarGridSpec(
        num_scalar_prefetch=2,
        grid=(nn, n_tiles),
        in_specs=[pl.BlockSpec((tm, f), lambda n, t, te, nv: (last(t, nv), 0)),
                  pl.BlockSpec((None, f, tn), lambda n, t, te, nv: (te[t], 0, n)),
                  pl.BlockSpec((None, 1, tn), lambda n, t, te, nv: (te[t], 0, n))],
        out_specs=pl.BlockSpec((None, tm * spo, LANES), lambda n, t, te, nv: (n, t, 0)),
        scratch_shapes=[pltpu.VMEM((f, tn), BF16)])
    return pl.pallas_call(
        functools.partial(_moe_down_kernel, tm=tm, spo=spo),
        grid_spec=grid_spec,
        out_shape=jax.ShapeDtypeStruct((nn, n_tiles * tm * spo, LANES), F32),
        compiler_params=_params(2),
    )(tile_expert, n_valid, act, w_down, b_down.reshape(e, 1, d))


def _combine_kernel(pos_ref, ys_ref, h_ref, tw_ref, g_ref, o_ref, on_ref, buf, sem,
                    *, tc, nn, spo):
    i = pl.program_id(0)

    def copy(r, kk, n):
        p = pos_ref[(i * tc + r) * TOP_K + kk]
        return pltpu.make_async_copy(
            ys_ref.at[n, pl.ds(pl.multiple_of(p * spo, spo), spo), :],
            buf.at[kk * nn + n, pl.ds(pl.multiple_of(r * spo, spo), spo), :],
            sem)

    def start(r, c):
        for kk in range(TOP_K):
            for n in range(nn):
                copy(r, kk, n).start()
        return c
    lax.fori_loop(0, tc, start, 0)

    def wait(r, c):
        for kk in range(TOP_K):
            for n in range(nn):
                copy(r, kk, n).wait()
        return c
    lax.fori_loop(0, tc, wait, 0)

    tw = tw_ref[...]
    cols = []
    for n in range(nn):
        for s in range(spo):
            acc = None
            for kk in range(TOP_K):
                piece = buf[kk * nn + n, pl.ds(s, tc, stride=spo), :] * tw[:, kk:kk + 1]
                acc = piece if acc is None else acc + piece
            cols.append(acc)
    y = jnp.concatenate(cols, axis=1)
    h2 = h_ref[...] + y
    o_ref[...] = h2
    hn = h2 * lax.rsqrt(jnp.mean(h2 * h2, axis=-1, keepdims=True) + EPS) * g_ref[...]
    on_ref[...] = hn.astype(on_ref.dtype)


def _combine(ys, pos, h, top_w, g_ple, *, tc):
    t, d = h.shape
    nn, _, _ = ys.shape
    spo = d // nn // LANES
    grid_spec = pltpu.PrefetchScalarGridSpec(
        num_scalar_prefetch=1,
        grid=(t // tc,),
        in_specs=[pl.BlockSpec(memory_space=pl.ANY),
                  pl.BlockSpec((tc, d), lambda i, pos: (i, 0)),
                  pl.BlockSpec((tc, LANES), lambda i, pos: (i, 0)),
                  pl.BlockSpec((1, d), lambda i, pos: (0, 0))],
        out_specs=[pl.BlockSpec((tc, d), lambda i, pos: (i, 0)),
                   pl.BlockSpec((tc, d), lambda i, pos: (i, 0))],
        scratch_shapes=[pltpu.VMEM((TOP_K * nn, tc * spo, LANES), F32),
                        pltpu.SemaphoreType.DMA(())])
    return pl.pallas_call(
        functools.partial(_combine_kernel, tc=tc, nn=nn, spo=spo),
        grid_spec=grid_spec,
        out_shape=[jax.ShapeDtypeStruct((t, d), F32),
                   jax.ShapeDtypeStruct((t, d), BF16)],
        compiler_params=_params(1),
    )(pos, ys, h, top_w, g_ple.reshape(1, d))


def _routing_plan(top_i, n_experts, tm, n_tiles):
    t = top_i.shape[0]
    eid = top_i[:, :TOP_K].reshape(-1)
    a = eid.shape[0]
    order = jnp.argsort(eid, stable=True).astype(jnp.int32)
    counts = jnp.sum(eid[:, None] == jnp.arange(n_experts, dtype=jnp.int32)[None, :],
                     axis=0, dtype=jnp.int32)
    tiles_per = (counts + tm - 1) // tm
    cum_tiles = jnp.cumsum(tiles_per)
    pad_off = (cum_tiles - tiles_per) * tm
    grp_start = jnp.cumsum(counts) - counts
    e_sorted = eid[order]
    dest = pad_off[e_sorted] + (jnp.arange(a, dtype=jnp.int32) - grp_start[e_sorted])
    pos = jnp.zeros((a,), jnp.int32).at[order].set(dest)
    row_token = jnp.zeros((n_tiles * tm,), jnp.int32).at[dest].set(order // TOP_K)
    n_valid = cum_tiles[-1].astype(jnp.int32)
    tile_ids = jnp.arange(n_tiles, dtype=jnp.int32)
    te = jnp.searchsorted(cum_tiles, tile_ids, side="right").astype(jnp.int32)
    te_last = te[jnp.maximum(n_valid - 1, 0)]
    te = jnp.where(tile_ids < n_valid, te, te_last)
    te = jnp.minimum(te, n_experts - 1)
    return row_token, pos, te, n_valid.reshape(1)


def _rope_tables(pos, dk):
    half = dk // 2
    inv = jnp.power(ROPE_THETA, -jnp.arange(half, dtype=F32) * 2.0 / dk)
    ang = pos.astype(F32)[:, None] * inv[None, :]
    cos = jnp.cos(ang)
    sin = jnp.sin(ang)
    reps = LANES // dk
    cos_t = jnp.tile(jnp.concatenate([cos, cos], axis=1), (1, reps))
    sin_t = jnp.tile(jnp.concatenate([-sin, sin], axis=1), (1, reps))
    return cos_t, sin_t


def _layer(x, pe16, p, cache_k, cache_v, page_table, dims, lam_init):
    b, s, db, t_new, n_heads, dk = dims
    t, d = x.shape
    tp = b * s
    dv = 2 * dk
    qk_w = 2 * n_heads * dk
    att_w = n_heads * dv
    gm_w = p["w_gmlp_out"].shape[0]
    n_experts = p["w_router"].shape[1]
    f = p["w_down"].shape[1]
    assert qk_w == att_w == gm_w, "segment widths must match the column block"
    tn = qk_w
    tm = _pick_tile(t, (768, 512, 384, 256, 128))

    w_in = p["w_in"].astype(BF16)
    xn = _rmsnorm(x, p["g_mix"], tm)

    g_qk = jnp.concatenate([jnp.tile(p["g_q"], qk_w // dk), jnp.tile(p["g_k"], qk_w // dk)]
                           ).reshape(1, 2 * qk_w)
    scale = jnp.concatenate([jnp.full((qk_w,), dk ** -0.5, F32), jnp.ones((qk_w,), F32)]
                            ).reshape(1, 2 * qk_w)
    lane = jnp.arange(LANES)
    blk = (lane[:, None] // dk == lane[None, :] // dk).astype(BF16)
    qk32, qk16 = _mm(xn, w_in, functools.partial(_ep_qk, dk=dk), tm=tm, tn=tn,
                     n_col_blocks=2, w_col0=0,
                     extras=[(g_qk, "col"), (scale, "col"), (p["cos"], "row"),
                             (p["sin"], "row"), (blk, "full")],
                     out_dtypes=[F32, BF16])
    v32, v16 = _mm(xn, w_in, _ep_copy2, tm=tm, tn=tn, n_col_blocks=1, w_col0=2,
                   out_dtypes=[F32, BF16])
    (u16,) = _mm(xn, w_in, _ep_gelu, tm=tm, tn=tn, n_col_blocks=1, w_col0=3,
                 out_dtypes=[BF16])
    (gv,) = _mm(xn, w_in, _ep_gelu_ln, tm=tm, tn=tn, n_col_blocks=1, w_col0=4,
                extras=[(p["g_ln_v"].reshape(1, gm_w), "col"),
                        (p["b_ln_v"].reshape(1, gm_w), "col")],
                out_dtypes=[F32])
    (gates16,) = _mm(xn, w_in, _ep_sigmoid, tm=tm, tn=tn, n_col_blocks=2 * d // tn, w_col0=5,
                     out_dtypes=[BF16])

    lam_vec = jnp.zeros((SUBLANES, LANES), F32).at[:4, :dk].set(
        jnp.stack([p["lam_q1"], p["lam_k1"], p["lam_q2"], p["lam_k2"]]))
    att_p = _prompt_attention(qk16, v16, lam_vec, p["g_subln"], b=b, s=s, h=n_heads,
                              dk=dk, lam_init=lam_init)
    q_s = qk16[tp:, :qk_w].reshape(db, t_new, 2 * n_heads, dk)
    eye = jnp.eye(2 * n_heads, dtype=BF16)
    qbd = jnp.einsum("btgd,gk->bgtkd", q_s, eye).reshape(db, 2 * n_heads * t_new, qk_w)
    n_pool, page = cache_k.shape[0], cache_k.shape[1]
    att_s = _sample_attention(
        qbd, qk32, v32, cache_k.reshape(n_pool, page, qk_w), cache_v.reshape(n_pool, page, att_w),
        page_table, lam_vec, p["g_subln"], new_row0=tp // t_new, n_heads=n_heads,
        t_new=t_new, lam_init=lam_init)
    att16 = jnp.concatenate([att_p, att_s.astype(BF16)], axis=0)

    gm16 = _gmlp(u16, gv, p["wmix"], p["bmix"], n_prompt_chunks=tp // LANES)

    merged = _merge(att16, gm16, p["w_att_out"].astype(BF16), p["w_gmlp_out"].astype(BF16),
                    gates16, tm=tm, tn=tn)
    (h,) = _mm(merged, p["w_o"].astype(BF16), _ep_residual, tm=tm, tn=tn,
               n_col_blocks=d // tn, extras=[(x, "tile")], out_dtypes=[F32])

    wr = jnp.zeros((d, LANES), F32).at[:, :n_experts].set(p["w_router"])
    wr_hi = wr.astype(BF16)
    wr_lo = (wr - wr_hi.astype(F32)).astype(BF16)
    b_pad = jnp.zeros((1, LANES), F32).at[0, :n_experts].set(p["b_router"])
    tmr = _pick_tile(t, (256, 128))
    hpk, top_i, top_w = _router(h, p["g_ffn"], wr_hi, wr_lo, b_pad, n_experts=n_experts, tm=tmr)
    spt = d // 2 // LANES
    tme = 256 if (t * TOP_K) % 256 == 0 else 128
    n_tiles = (t * TOP_K) // tme + n_experts
    row_token, pos, te, n_valid = _routing_plan(top_i, n_experts, tme, n_tiles)
    xs = _gather_rows(hpk, row_token, n_valid, n_tiles=n_tiles, tm=tme, spt=spt)
    act = _moe_up(xs, p["w_gate_up"], p["b_gate_up"], te, n_valid,
                  n_tiles=n_tiles, tm=tme, spt=spt, fc=min(512, f))
    ys = _moe_down(act, p["w_down"], p["b_down"], te, n_valid, n_tiles=n_tiles, tm=tme,
                   tn=min(1024, d))
    h2, h2n = _combine(ys, pos, h, top_w, p["g_ple"], tc=LANES)

    (out,) = _mm(h2n, p["w_ple_gate"].astype(BF16), _ep_ple, tm=tm, tn=tn, n_col_blocks=d // tn,
                 extras=[(h2, "tile"), (pe16, "row"), (p["w_ple_proj"].astype(BF16), "colmat")],
                 out_dtypes=[F32])
    k32 = qk32[:, qk_w:]
    return out, k32, v32, gv


def kernel(x_prompt, x_sample, cache_k, cache_v, page_table, p_prompt, p_sample, g_mix, w_in, g_q, g_k, lam_q1, lam_k1, lam_q2, lam_k2, g_subln, g_ln_v, b_ln_v, w_spatial, b_spatial, w_att_out, w_gmlp_out, w_o, g_ffn, w_router, b_router, w_gate_up, b_gate_up, w_down, b_down, g_ple, w_ple_gate, w_ple_proj):
    b, s, d = x_prompt.shape
    db, t_new, _ = x_sample.shape
    depth = w_in.shape[0]
    page = cache_k.shape[2]
    n_heads, dv = cache_v.shape[3], cache_v.shape[4]
    dk = cache_k.shape[4]
    n_pages = page_table.shape[1]
    tp, ts = b * s, db * t_new
    groups, chunk = w_spatial.shape[1], w_spatial.shape[2]
    assert chunk == LANES and dv == LANES and ts % LANES == 0 and s % LANES == 0

    pos_all = jnp.concatenate([jnp.tile(jnp.arange(s), b),
                               jnp.tile(n_pages * page + jnp.arange(t_new), db)])
    cos_t, sin_t = _rope_tables(pos_all, dk)
    x = jnp.concatenate([x_prompt.reshape(tp, d), x_sample.reshape(ts, d)], axis=0)
    tril = jnp.tril(jnp.ones((chunk, chunk), F32))
    seq_per_chunk = chunk // t_new

    kp, vp, ks, vs, gs = [], [], [], [], []
    for i in range(depth):
        lam_init = 0.8 - 0.6 * math.exp(-0.3 * i)
        ws = w_spatial[i] * tril
        ws_s = jnp.einsum("ab,gts->gatbs", jnp.eye(seq_per_chunk, dtype=F32),
                          ws[:, :t_new, :t_new]).reshape(groups, chunk, chunk)
        wmix = jnp.stack([ws, ws_s]).astype(BF16)
        b_p = b_spatial[i]
        b_s = jnp.tile(b_spatial[i][:, :t_new], (1, seq_per_chunk))
        bmix = jnp.broadcast_to(jnp.stack([b_p, b_s])[..., None],
                                (2, groups, chunk, LANES)).astype(F32)
        p = dict(g_mix=g_mix[i], w_in=w_in[i], g_q=g_q[i], g_k=g_k[i], g_subln=g_subln[i],
                 g_ln_v=g_ln_v[i], b_ln_v=b_ln_v[i], w_att_out=w_att_out[i],
                 w_gmlp_out=w_gmlp_out[i], w_o=w_o[i], g_ffn=g_ffn[i], w_router=w_router[i],
                 b_router=b_router[i], w_gate_up=w_gate_up[i], b_gate_up=b_gate_up[i],
                 w_down=w_down[i], b_down=b_down[i], g_ple=g_ple[i], w_ple_gate=w_ple_gate[i],
                 w_ple_proj=w_ple_proj[i], lam_q1=lam_q1[i], lam_k1=lam_k1[i],
                 lam_q2=lam_q2[i], lam_k2=lam_k2[i], cos=cos_t, sin=sin_t,
                 wmix=wmix, bmix=bmix)
        pe16 = jnp.concatenate([p_prompt[i].reshape(tp, -1), p_sample[i].reshape(ts, -1)],
                               axis=0).astype(BF16)
        x, k32, v32, gv = _layer(x, pe16, p, cache_k[i], cache_v[i], page_table,
                                 (b, s, db, t_new, n_heads, dk), lam_init)
        kp.append(k32[:tp].reshape(b, s, 2 * n_heads, dk))
        vp.append(v32[:tp].reshape(b, s, n_heads, dv))
        ks.append(k32[tp:].reshape(db, t_new, 2 * n_heads, dk))
        vs.append(v32[tp:].reshape(db, t_new, n_heads, dv))
        gs.append(gv[tp:].reshape(db, t_new, -1))
    return (x[:tp].reshape(b, s, d), x[tp:].reshape(db, t_new, d),
            jnp.stack(kp), jnp.stack(vp), jnp.stack(ks), jnp.stack(vs), jnp.stack(gs))
```

```python
import functools
import math

import jax
import jax.numpy as jnp
from jax import lax
from jax.experimental import pallas as pl
from jax.experimental.pallas import tpu as pltpu

EPS = 1e-6
NEG_INF = -1e30
MASKED = -3.0e38
ROPE_THETA = 10000.0
SWIGLU_ALPHA = 1.702
SWIGLU_LIMIT = 7.0
TOP_K = 4
LANES = 128
SUBLANES = 8
VMEM_LIMIT_BYTES = 56 * 1024 * 1024
BF16 = jnp.bfloat16
F32 = jnp.float32


def _pick_tile(n, candidates):
    for c in candidates:
        if n % c == 0:
            return c
    raise ValueError(f"no tile in {candidates} divides {n}")


def _params(n_axes):
    return pltpu.CompilerParams(
        dimension_semantics=("arbitrary",) * n_axes,
        vmem_limit_bytes=VMEM_LIMIT_BYTES)


def _gelu(x):
    c = math.sqrt(2.0 / math.pi)
    return x * (0.5 * (1.0 + jnp.tanh(c * (x + 0.044715 * (x * x * x)))))


def _sigmoid(x):
    return 1.0 / (1.0 + jnp.exp(-x))


def _rmsnorm_kernel(x_ref, g_ref, o_ref):
    x = x_ref[...]
    y = x * lax.rsqrt(jnp.mean(x * x, axis=-1, keepdims=True) + EPS) * g_ref[...]
    o_ref[...] = y.astype(o_ref.dtype)


def _rmsnorm(x, g, tm):
    m, d = x.shape
    return pl.pallas_call(
        _rmsnorm_kernel,
        grid=(m // tm,),
        in_specs=[pl.BlockSpec((tm, d), lambda i: (i, 0)),
                  pl.BlockSpec((1, d), lambda i: (0, 0))],
        out_specs=pl.BlockSpec((tm, d), lambda i: (i, 0)),
        out_shape=jax.ShapeDtypeStruct((m, d), BF16),
        compiler_params=_params(1),
    )(x, g.reshape(1, d))


def _mm_kernel(*refs, epilogue, n_extra):
    x_ref, w_ref = refs[0], refs[1]
    extra = refs[2:2 + n_extra]
    outs = refs[2 + n_extra:]
    acc = jnp.dot(x_ref[...], w_ref[...], preferred_element_type=F32)
    epilogue(acc, extra, outs)


def _mm(x, w, epilogue, *, tm, tn, n_col_blocks, w_col0=0, extras=(), out_dtypes):
    m, k = x.shape
    nj, ni = n_col_blocks, m // tm
    in_specs = [pl.BlockSpec((tm, k), lambda j, i: (i, 0)),
                pl.BlockSpec((k, tn), lambda j, i: (0, w_col0 + j))]
    args = [x, w]
    for arr, kind in extras:
        if kind == "tile":
            spec = pl.BlockSpec((tm, tn), lambda j, i: (i, j))
        elif kind == "tile2":
            spec = pl.BlockSpec((tm, tn), lambda j, i: (i, nj + j))
        elif kind == "row":
            spec = pl.BlockSpec((tm, arr.shape[1]), lambda j, i: (i, 0))
        elif kind == "col":
            spec = pl.BlockSpec((1, tn), lambda j, i: (0, j))
        elif kind == "colmat":
            spec = pl.BlockSpec((arr.shape[0], tn), lambda j, i: (0, j))
        elif kind == "full":
            spec = pl.BlockSpec(arr.shape, lambda j, i, nd=arr.ndim: (0,) * nd)
        else:
            raise ValueError(kind)
        in_specs.append(spec)
        args.append(arr)
    out_specs = [pl.BlockSpec((tm, tn), lambda j, i: (i, j)) for _ in out_dtypes]
    out_shape = [jax.ShapeDtypeStruct((m, nj * tn), dt) for dt in out_dtypes]
    res = pl.pallas_call(
        functools.partial(_mm_kernel, epilogue=epilogue, n_extra=len(extras)),
        grid=(nj, ni),
        in_specs=in_specs,
        out_specs=out_specs,
        out_shape=out_shape,
        compiler_params=_params(2),
    )(*args)
    return res


def _ep_qk(acc, extra, outs, *, dk):
    g_ref, scale_ref, cos_ref, sin_ref, blk_ref = extra
    o32, o16 = outs
    cos = cos_ref[...]
    sin = sin_ref[...]
    blk = blk_ref[...]
    half = dk // 2
    lane = lax.broadcasted_iota(jnp.int32, (acc.shape[0], LANES), 1)
    upper = (lane & half) != 0
    for cb in range(acc.shape[1] // LANES):
        sl = slice(cb * LANES, (cb + 1) * LANES)
        z = acc[:, sl]
        ss = jnp.dot((z * z).astype(BF16), blk, preferred_element_type=F32)
        y = z * lax.rsqrt(ss * (1.0 / dk) + EPS) * g_ref[:, sl]
        partner = jnp.where(upper, pltpu.roll(y, half, 1), pltpu.roll(y, LANES - half, 1))
        r = y * cos + partner * sin
        o32[:, sl] = r
        o16[:, sl] = (r * scale_ref[:, sl]).astype(BF16)


def _ep_copy2(acc, extra, outs):
    outs[0][...] = acc
    outs[1][...] = acc.astype(BF16)


def _ep_gelu(acc, extra, outs):
    outs[0][...] = _gelu(acc).astype(outs[0].dtype)


def _ep_gelu_ln(acc, extra, outs):
    g_ref, b_ref = extra
    y = _gelu(acc)
    yc = y - jnp.mean(y, axis=-1, keepdims=True)
    z = yc * lax.rsqrt(jnp.mean(yc * yc, axis=-1, keepdims=True) + EPS)
    outs[0][...] = z * g_ref[...] + b_ref[...]


def _ep_sigmoid(acc, extra, outs):
    outs[0][...] = _sigmoid(acc).astype(outs[0].dtype)


def _ep_residual(acc, extra, outs):
    outs[0][...] = extra[0][...] + acc


def _ep_ple(acc, extra, outs):
    h_ref, pe_ref, wp_ref = extra
    proj = jnp.dot(pe_ref[...], wp_ref[...], preferred_element_type=F32)
    outs[0][...] = h_ref[...] + _sigmoid(acc) * proj


def _diff_lambda(lam_ref, lam_init):
    v = lam_ref[...]
    a = jnp.sum(v[0:1, :] * v[1:2, :], axis=-1, keepdims=True)
    b = jnp.sum(v[2:3, :] * v[3:4, :], axis=-1, keepdims=True)
    return jnp.exp(a) - jnp.exp(b) + lam_init


def _subln(o, g, lam_init):
    y = o * lax.rsqrt(jnp.mean(o * o, axis=-1, keepdims=True) + EPS) * g
    return y * (1.0 - lam_init)


def _prompt_attn_kernel(q_ref, k_ref, v_ref, lam_ref, g_ref, o_ref,
                        m_sc, l_sc, acc_sc, *, tq, tk, rg, dk, lam_init):
    qi = pl.program_id(2)
    q0 = qi * tq
    q = q_ref[...]
    lane = lax.broadcasted_iota(jnp.int32, q.shape, 1)
    zero = jnp.zeros_like(q)
    q2 = jnp.concatenate([jnp.where(lane < dk, q, zero),
                          jnp.where(lane >= dk, q, zero)], axis=0)
    m_sc[...] = jnp.full(m_sc.shape, NEG_INF, F32)
    l_sc[...] = jnp.zeros(l_sc.shape, F32)
    acc_sc[...] = jnp.zeros(acc_sc.shape, F32)

    def step(j, masked):
        start = pl.multiple_of(j * tk, tk)
        kj = k_ref[pl.ds(start, tk), :]
        vj = v_ref[pl.ds(start, tk), :]
        for r in range(2 * tq // rg):
            rows = pl.ds(r * rg, rg)
            s = lax.dot_general(q2[r * rg:(r + 1) * rg], kj, (((1,), (1,)), ((), ())),
                                preferred_element_type=F32)
            if masked:
                row = lax.broadcasted_iota(jnp.int32, (rg, tk), 0) + (r * rg) % tq
                col = lax.broadcasted_iota(jnp.int32, (rg, tk), 1)
                s = jnp.where(col - row <= q0 - start, s, NEG_INF)
            m_old = m_sc[rows, :]
            m_new = jnp.maximum(m_old, jnp.max(s, axis=-1, keepdims=True))
            alpha = jnp.exp(m_old - m_new)
            p = jnp.exp(s - m_new)
            l_sc[rows, :] = alpha * l_sc[rows, :] + jnp.sum(p, axis=-1, keepdims=True)
            acc_sc[rows, :] = alpha * acc_sc[rows, :] + jnp.dot(
                p.astype(BF16), vj, preferred_element_type=F32)
            m_sc[rows, :] = m_new

    def body(j, carry):
        step(j, False)
        return carry

    n_full = q0 // tk
    lax.fori_loop(0, n_full, body, 0)
    step(n_full, True)

    lam = _diff_lambda(lam_ref, lam_init)
    o = acc_sc[...] / l_sc[...]
    od = o[:tq] - lam * o[tq:]
    o_ref[...] = _subln(od, g_ref[...], lam_init).astype(o_ref.dtype)


def _prompt_attention(qk16, v16, lam_vec, g_subln, *, b, s, h, dk, lam_init):
    dv = 2 * dk
    assert dv == LANES
    tq = _pick_tile(s, (256, 128))
    tk = _pick_tile(s, (1024, 512, 256, 128))
    assert tk % tq == 0
    nq = s // tq
    return pl.pallas_call(
        functools.partial(_prompt_attn_kernel, tq=tq, tk=tk, rg=min(128, tq), dk=dk,
                          lam_init=lam_init),
        grid=(b, h, nq),
        in_specs=[pl.BlockSpec((tq, LANES), lambda bi, hi, qi: (bi * nq + qi, hi)),
                  pl.BlockSpec((s, LANES), lambda bi, hi, qi: (bi, h + hi)),
                  pl.BlockSpec((s, LANES), lambda bi, hi, qi: (bi, hi)),
                  pl.BlockSpec((SUBLANES, LANES), lambda bi, hi, qi: (0, 0)),
                  pl.BlockSpec((1, LANES), lambda bi, hi, qi: (0, 0))],
        out_specs=pl.BlockSpec((tq, LANES), lambda bi, hi, qi: (bi * nq + qi, hi)),
        out_shape=jax.ShapeDtypeStruct((b * s, h * dv), BF16),
        scratch_shapes=[pltpu.VMEM((2 * tq, 1), F32),
                        pltpu.VMEM((2 * tq, 1), F32),
                        pltpu.VMEM((2 * tq, LANES), F32)],
        compiler_params=_params(3),
    )(qk16, qk16, v16, lam_vec, g_subln.reshape(1, dv))


def _sample_attn_kernel(pt_ref, qbd_ref, knew_ref, vnew_ref, lam_ref, g_ref, *rest,
                        pp, page, n_heads, t_new, lam_init):
    kpages = rest[:pp]
    vpages = rest[pp:2 * pp]
    o_ref = rest[2 * pp]
    kb_sc, vb_sc, m_sc, l_sc, acc_sc = rest[2 * pp + 1:]
    step = pl.program_id(1)
    n_steps = pl.num_programs(1)
    dv = LANES
    rph = 2 * t_new

    @pl.when(step == 0)
    def _():
        m_sc[...] = jnp.full(m_sc.shape, NEG_INF, F32)
        l_sc[...] = jnp.zeros(l_sc.shape, F32)
        acc_sc[...] = jnp.zeros(acc_sc.shape, F32)

    qbd = qbd_ref[...]

    def update(s, v_of_head):
        m_old = m_sc[...]
        m_new = jnp.maximum(m_old, jnp.max(s, axis=-1, keepdims=True))
        alpha = jnp.exp(m_old - m_new)
        p = jnp.exp(s - m_new)
        l_sc[...] = alpha * l_sc[...] + jnp.sum(p, axis=-1, keepdims=True)
        pb = p.astype(BF16)
        for hh in range(n_heads):
            rows = pl.ds(hh * rph, rph)
            acc_sc[rows, :] = alpha[hh * rph:(hh + 1) * rph] * acc_sc[rows, :] + jnp.dot(
                pb[hh * rph:(hh + 1) * rph], v_of_head(hh), preferred_element_type=F32)
        m_sc[...] = m_new

    for p_i in range(pp):
        kb_sc[:, pl.ds(p_i * page, page)] = kpages[p_i][...].astype(BF16)
        for hh in range(n_heads):
            vb_sc[hh, pl.ds(p_i * page, page), :] = (
                vpages[p_i][pl.ds(hh, page, stride=n_heads), :].astype(BF16))
    s = jnp.dot(qbd, kb_sc[...], preferred_element_type=F32)
    update(s, lambda hh: vb_sc[hh])

    @pl.when(step == n_steps - 1)
    def _():
        sn = jnp.dot(qbd, knew_ref[...].astype(BF16), preferred_element_type=F32)
        row = lax.broadcasted_iota(jnp.int32, sn.shape, 0)
        col = lax.broadcasted_iota(jnp.int32, sn.shape, 1)
        keep = col <= (row & (t_new - 1))
        update(jnp.where(keep, sn, NEG_INF), lambda hh: vnew_ref[hh].astype(BF16))

        lam = _diff_lambda(lam_ref, lam_init)
        o = acc_sc[...] / l_sc[...]
        for hh in range(n_heads):
            r1 = hh * rph
            od = o[r1:r1 + t_new] - lam * o[r1 + t_new:r1 + rph]
            o_ref[:, hh * dv:(hh + 1) * dv] = _subln(od, g_ref[...], lam_init)


def _sample_attention(qbd, k_new_t, v_new, cache_kt, cache_v2, layer, page_table, lam_vec,
                      g_subln, *, n_heads, t_new, lam_init):
    db, rows, width = qbd.shape
    page = cache_kt.shape[3]
    n_pages = page_table.shape[1]
    pp = _pick_tile(n_pages, (8, 4, 2, 1))
    n_steps = n_pages // pp
    dv = LANES
    assert page == LANES and t_new & (t_new - 1) == 0 and t_new % SUBLANES == 0

    def kspec(p_i):
        return pl.BlockSpec((None, None, width, page),
                            lambda b, st, pt: (layer, pt[b, st * pp + p_i], 0, 0))

    def vspec(p_i):
        return pl.BlockSpec((None, None, page * n_heads, dv),
                            lambda b, st, pt: (layer, pt[b, st * pp + p_i], 0, 0))

    in_specs = [pl.BlockSpec((None, rows, width), lambda b, st, pt: (b, 0, 0)),
                pl.BlockSpec((None, width, LANES), lambda b, st, pt: (b, 0, 0)),
                pl.BlockSpec((None, n_heads, LANES, dv), lambda b, st, pt: (b, 0, 0, 0)),
                pl.BlockSpec((SUBLANES, LANES), lambda b, st, pt: (0, 0)),
                pl.BlockSpec((1, dv), lambda b, st, pt: (0, 0))]
    in_specs += [kspec(p_i) for p_i in range(pp)]
    in_specs += [vspec(p_i) for p_i in range(pp)]
    grid_spec = pltpu.PrefetchScalarGridSpec(
        num_scalar_prefetch=1,
        grid=(db, n_steps),
        in_specs=in_specs,
        out_specs=pl.BlockSpec((t_new, n_heads * dv), lambda b, st, pt: (b, 0)),
        scratch_shapes=[pltpu.VMEM((width, pp * page), BF16),
                        pltpu.VMEM((n_heads, pp * page, dv), BF16),
                        pltpu.VMEM((rows, 1), F32),
                        pltpu.VMEM((rows, 1), F32),
                        pltpu.VMEM((rows, dv), F32)])
    return pl.pallas_call(
        functools.partial(_sample_attn_kernel, pp=pp, page=page, n_heads=n_heads,
                          t_new=t_new, lam_init=lam_init),
        grid_spec=grid_spec,
        out_shape=jax.ShapeDtypeStruct((db * t_new, n_heads * dv), F32),
        compiler_params=_params(2),
    )(page_table, qbd, k_new_t, v_new, lam_vec, g_subln.reshape(1, dv),
      *([cache_kt] * pp), *([cache_v2] * pp))


def _gmlp_kernel(u_ref, gv_ref, w_ref, b_ref, o_ref, *, groups):
    for g in range(groups):
        sl = slice(g * LANES, (g + 1) * LANES)
        s = jnp.dot(w_ref[g], gv_ref[:, sl].astype(BF16), preferred_element_type=F32)
        s = s + b_ref[g]
        o_ref[:, sl] = (u_ref[:, sl].astype(F32) * s).astype(o_ref.dtype)


def _gmlp(u16, gv, wmix, bmix, *, n_prompt_chunks):
    t, width = gv.shape
    groups, chunk = wmix.shape[1], wmix.shape[2]
    assert chunk == LANES and width == groups * LANES
    sel = lambda c: (jnp.where(c < n_prompt_chunks, 0, 1), 0, 0, 0)
    return pl.pallas_call(
        functools.partial(_gmlp_kernel, groups=groups),
        grid=(t // chunk,),
        in_specs=[pl.BlockSpec((chunk, width), lambda c: (c, 0)),
                  pl.BlockSpec((chunk, width), lambda c: (c, 0)),
                  pl.BlockSpec((None, groups, chunk, chunk), sel),
                  pl.BlockSpec((None, groups, chunk, LANES), sel)],
        out_specs=pl.BlockSpec((chunk, width), lambda c: (c, 0)),
        out_shape=jax.ShapeDtypeStruct((t, width), BF16),
        compiler_params=_params(1),
    )(u16, gv, wmix, bmix)


def _merge_kernel(att_ref, gm_ref, wa_ref, wg_ref, ga_ref, gg_ref, o_ref):
    a = jnp.dot(att_ref[...], wa_ref[...], preferred_element_type=F32)
    g = jnp.dot(gm_ref[...], wg_ref[...], preferred_element_type=F32)
    o_ref[...] = (ga_ref[...].astype(F32) * a + gg_ref[...].astype(F32) * g).astype(o_ref.dtype)


def _merge(att16, gm16, wa, wg, gates16, *, tm, tn):
    t, ka = att16.shape
    kg = gm16.shape[1]
    d = wa.shape[1]
    nj = d // tn
    return pl.pallas_call(
        _merge_kernel,
        grid=(nj, t // tm),
        in_specs=[pl.BlockSpec((tm, ka), lambda j, i: (i, 0)),
                  pl.BlockSpec((tm, kg), lambda j, i: (i, 0)),
                  pl.BlockSpec((ka, tn), lambda j, i: (0, j)),
                  pl.BlockSpec((kg, tn), lambda j, i: (0, j)),
                  pl.BlockSpec((tm, tn), lambda j, i: (i, j)),
                  pl.BlockSpec((tm, tn), lambda j, i: (i, nj + j))],
        out_specs=pl.BlockSpec((tm, tn), lambda j, i: (i, j)),
        out_shape=jax.ShapeDtypeStruct((t, d), BF16),
        compiler_params=_params(2),
    )(att16, gm16, wa, wg, gates16, gates16)


def _router_kernel(h_ref, g_ref, wh_ref, wl_ref, b_ref, xn_ref, ti_ref, tw_ref,
                   *, n_experts):
    x = h_ref[...]
    xn = x * lax.rsqrt(jnp.mean(x * x, axis=-1, keepdims=True) + EPS) * g_ref[...]
    tm, d = xn.shape
    xh = xn.astype(BF16)
    xl = (xn - xh.astype(F32)).astype(BF16)
    wh = wh_ref[...]
    logits = (jnp.dot(xh, wh, preferred_element_type=F32)
              + jnp.dot(xl, wh, preferred_element_type=F32)
              + jnp.dot(xh, wl_ref[...], preferred_element_type=F32)) + b_ref[...]
    lane = lax.broadcasted_iota(jnp.int32, logits.shape, 1)
    lane_f = lane.astype(F32)
    cur = jnp.where(lane < n_experts, logits, MASKED)
    vals, idxs = [], []
    for _ in range(TOP_K):
        m = jnp.max(cur, axis=-1, keepdims=True)
        idx = jnp.min(jnp.where(cur == m, lane_f, float(LANES)), axis=-1, keepdims=True)
        vals.append(m)
        idxs.append(idx)
        cur = jnp.where(lane_f == idx, MASKED, cur)
    es = [jnp.exp(v - vals[0]) for v in vals]
    tot = es[0]
    for e in es[1:]:
        tot = tot + e
    ti = jnp.zeros(logits.shape, F32)
    tw = jnp.zeros(logits.shape, F32)
    for kk in range(TOP_K):
        ti = jnp.where(lane == kk, idxs[kk], ti)
        tw = jnp.where(lane == kk, es[kk] / tot, tw)
    ti_ref[...] = ti.astype(jnp.int32)
    tw_ref[...] = tw
    spt = d // LANES
    for s in range(spt):
        xn_ref[pl.ds(s, tm, stride=spt), :] = xn[:, s * LANES:(s + 1) * LANES]


def _router(h, g_ffn, wr_hi, wr_lo, b_pad, *, n_experts, tm):
    t, d = h.shape
    spt = d // LANES
    return pl.pallas_call(
        functools.partial(_router_kernel, n_experts=n_experts),
        grid=(t // tm,),
        in_specs=[pl.BlockSpec((tm, d), lambda i: (i, 0)),
                  pl.BlockSpec((1, d), lambda i: (0, 0)),
                  pl.BlockSpec((d, LANES), lambda i: (0, 0)),
                  pl.BlockSpec((d, LANES), lambda i: (0, 0)),
                  pl.BlockSpec((1, LANES), lambda i: (0, 0))],
        out_specs=[pl.BlockSpec((tm * spt, LANES), lambda i: (i, 0)),
                   pl.BlockSpec((tm, LANES), lambda i: (i, 0)),
                   pl.BlockSpec((tm, LANES), lambda i: (i, 0))],
        out_shape=[jax.ShapeDtypeStruct((t * spt, LANES), F32),
                   jax.ShapeDtypeStruct((t, LANES), jnp.int32),
                   jax.ShapeDtypeStruct((t, LANES), F32)],
        compiler_params=_params(1),
    )(h, g_ffn.reshape(1, d), wr_hi, wr_lo, b_pad)


def _gather_kernel(idx_ref, nv_ref, src_ref, o_ref, buf, sem, *, tm, spt):
    t = pl.program_id(0)
    valid = t < nv_ref[0]

    def copy(r):
        tok = idx_ref[t * tm + r]
        return pltpu.make_async_copy(
            src_ref.at[pl.ds(pl.multiple_of(tok * spt, spt), spt), :],
            buf.at[pl.ds(pl.multiple_of(r * spt, spt), spt), :],
            sem)

    @pl.when(valid)
    def _():
        def start(r, c):
            copy(r).start()
            return c
        lax.fori_loop(0, tm, start, 0)

        def wait(r, c):
            copy(r).wait()
            return c
        lax.fori_loop(0, tm, wait, 0)
        for s in range(spt):
            o_ref[:, s * LANES:(s + 1) * LANES] = buf[pl.ds(s, tm, stride=spt), :].astype(o_ref.dtype)

    @pl.when(jnp.logical_not(valid))
    def _():
        o_ref[...] = jnp.zeros(o_ref.shape, o_ref.dtype)


def _gather_rows(src, idx, n_valid_tiles, *, n_tiles, tm, spt):
    grid_spec = pltpu.PrefetchScalarGridSpec(
        num_scalar_prefetch=2,
        grid=(n_tiles,),
        in_specs=[pl.BlockSpec(memory_space=pl.ANY)],
        out_specs=pl.BlockSpec((tm, spt * LANES), lambda t, idx, nv: (t, 0)),
        scratch_shapes=[pltpu.VMEM((tm * spt, LANES), src.dtype),
                        pltpu.SemaphoreType.DMA(())])
    return pl.pallas_call(
        functools.partial(_gather_kernel, tm=tm, spt=spt),
        grid_spec=grid_spec,
        out_shape=jax.ShapeDtypeStruct((n_tiles * tm, spt * LANES), BF16),
        compiler_params=_params(1),
    )(idx, n_valid_tiles, src)


def _new_group(te_ref, t):
    prev = te_ref[jnp.maximum(t - 1, 0)]
    return jnp.logical_or(t == 0, te_ref[t] != prev)


def _moe_up_kernel(te_ref, nv_ref, xs_ref, wg_ref, wl_ref, bg_ref, bl_ref, o_ref,
                   wg_sc, wl_sc):
    t = pl.program_id(1)
    valid = t < nv_ref[0]

    @pl.when(jnp.logical_and(valid, _new_group(te_ref, t)))
    def _():
        wg_sc[...] = wg_ref[...].astype(BF16)
        wl_sc[...] = wl_ref[...].astype(BF16)

    @pl.when(valid)
    def _():
        x = xs_ref[...]
        gl =jnp.dot(x, wg_sc[...], preferred_element_type=F32) + bg_ref[...]
        ln = jnp.dot(x, wl_sc[...], preferred_element_type=F32) + bl_ref[...]
        glu = jnp.minimum(gl, SWIGLU_LIMIT)
        lin = jnp.clip(ln, -SWIGLU_LIMIT, SWIGLU_LIMIT)
        act = glu * _sigmoid(SWIGLU_ALPHA * glu) * (lin + 1.0)
        o_ref[...] = act.astype(o_ref.dtype)

    @pl.when(jnp.logical_not(valid))
    def _():
        o_ref[...] = jnp.zeros(o_ref.shape, o_ref.dtype)


def _moe_up(xs, w_gate_up, b_gate_up, tile_expert, n_valid, *, n_tiles, tm, fc):
    e, d, f2 = w_gate_up.shape
    f = f2 // 2
    nc = f // fc
    last = lambda t, nv: jnp.minimum(t, nv[0] - 1)
    grid_spec = pltpu.PrefetchScalarGridSpec(
        num_scalar_prefetch=2,
        grid=(nc, n_tiles),
        in_specs=[pl.BlockSpec((tm, d), lambda c, t, te, nv: (last(t, nv), 0)),
                  pl.BlockSpec((None, d, fc), lambda c, t, te, nv: (te[t], 0, c)),
                  pl.BlockSpec((None, d, fc), lambda c, t, te, nv: (te[t], 0, nc + c)),
                  pl.BlockSpec((None, 1, fc), lambda c, t, te, nv: (te[t], 0, c)),
                  pl.BlockSpec((None, 1, fc), lambda c, t, te, nv: (te[t], 0, nc + c))],
        out_specs=pl.BlockSpec((tm, fc), lambda c, t, te, nv: (t, c)),
        scratch_shapes=[pltpu.VMEM((d, fc), BF16), pltpu.VMEM((d, fc), BF16)])
    return pl.pallas_call(
        _moe_up_kernel,
        grid_spec=grid_spec,
        out_shape=jax.ShapeDtypeStruct((n_tiles * tm, f), BF16),
        compiler_params=_params(2),
    )(tile_expert, n_valid, xs, w_gate_up, w_gate_up,
      b_gate_up.reshape(e, 1, f2), b_gate_up.reshape(e, 1, f2))


def _moe_down_kernel(te_ref, nv_ref, a_ref, w_ref, b_ref, o_ref, w_sc, *, tm, spo):
    t = pl.program_id(1)
    valid = t < nv_ref[0]

    @pl.when(jnp.logical_and(valid, _new_group(te_ref, t)))
    def _():
        w_sc[...] = w_ref[...].astype(BF16)

    @pl.when(valid)
    def _():
        y = jnp.dot(a_ref[...], w_sc[...], preferred_element_type=F32) + b_ref[...]
        for s in range(y.shape[1] // LANES):
            o_ref[pl.ds(s, tm, stride=spo), :] = y[:, s * LANES:(s + 1) * LANES]

    @pl.when(jnp.logical_not(valid))
    def _():
        o_ref[...] = jnp.zeros(o_ref.shape, o_ref.dtype)


def _moe_down(act, w_down, b_down, tile_expert, n_valid, *, n_tiles, tm, tn):
    e, f, d = w_down.shape
    nn = d // tn
    spo = tn // LANES
    last = lambda t, nv: jnp.minimum(t, nv[0] - 1)
    grid_spec = pltpu.PrefetchScalarGridSpec(
        num_scalar_prefetch=2,
        grid=(nn, n_tiles),
        in_specs=[pl.BlockSpec((tm, f), lambda n, t, te, nv: (last(t, nv), 0)),
                  pl.BlockSpec((None, f, tn), lambda n, t, te, nv: (te[t], 0, n)),
                  pl.BlockSpec((None, 1, tn), lambda n, t, te, nv: (te[t], 0, n))],
        out_specs=pl.BlockSpec((None, tm * spo, LANES), lambda n, t, te, nv: (n, t, 0)),
        scratch_shapes=[pltpu.VMEM((f, tn), BF16)])
    return pl.pallas_call(
        functools.partial(_moe_down_kernel, tm=tm, spo=spo),
        grid_spec=grid_spec,
        out_shape=jax.ShapeDtypeStruct((nn, n_tiles * tm * spo, LANES), F32),
        compiler_params=_params(2),
    )(tile_expert, n_valid, act, w_down, b_down.reshape(e, 1, d))


def _combine_kernel(pos_ref, ys_ref, h_ref, tw_ref, g_ref, o_ref, on_ref, buf, sem,
                    *, tc, nn, spo):
    i = pl.program_id(0)

    def copy(r, kk, n):
        p = pos_ref[(i * tc + r) * TOP_K + kk]
        return pltpu.make_async_copy(
            ys_ref.at[n, pl.ds(pl.multiple_of(p * spo, spo), spo), :],
            buf.at[kk * nn + n, pl.ds(pl.multiple_of(r * spo, spo), spo), :],
            sem)

    def start(r, c):
        for kk in range(TOP_K):
            for n in range(nn):
                copy(r, kk, n).start()
        return c
    lax.fori_loop(0, tc, start, 0)

    def wait(r, c):
        for kk in range(TOP_K):
            for n in range(nn):
                copy(r, kk, n).wait()
        return c
    lax.fori_loop(0, tc, wait, 0)

    tw = tw_ref[...]
    cols = []
    for n in range(nn):
        for s in range(spo):
            acc = None
            for kk in range(TOP_K):
                piece = buf[kk * nn + n, pl.ds(s, tc, stride=spo), :] * tw[:, kk:kk + 1]
                acc = piece if acc is None else acc + piece
            cols.append(acc)
    y = jnp.concatenate(cols, axis=1)
    h2 = h_ref[...] + y
    o_ref[...] = h2
    hn = h2 * lax.rsqrt(jnp.mean(h2 * h2, axis=-1, keepdims=True) + EPS) * g_ref[...]
    on_ref[...] = hn.astype(on_ref.dtype)


def _combine(ys, pos, h, top_w, g_ple, *, tc):
    t, d = h.shape
    nn, _, _ = ys.shape
    spo = d // nn // LANES
    grid_spec = pltpu.PrefetchScalarGridSpec(
        num_scalar_prefetch=1,
        grid=(t // tc,),
        in_specs=[pl.BlockSpec(memory_space=pl.ANY),
                  pl.BlockSpec((tc, d), lambda i, pos: (i, 0)),
                  pl.BlockSpec((tc, LANES), lambda i, pos: (i, 0)),
                  pl.BlockSpec((1, d), lambda i, pos: (0, 0))],
        out_specs=[pl.BlockSpec((tc, d), lambda i, pos: (i, 0)),
                   pl.BlockSpec((tc, d), lambda i, pos: (i, 0))],
        scratch_shapes=[pltpu.VMEM((TOP_K * nn, tc * spo, LANES), F32),
                        pltpu.SemaphoreType.DMA(())])
    return pl.pallas_call(
        functools.partial(_combine_kernel, tc=tc, nn=nn, spo=spo),
        grid_spec=grid_spec,
        out_shape=[jax.ShapeDtypeStruct((t, d), F32),
                   jax.ShapeDtypeStruct((t, d), BF16)],
        compiler_params=_params(1),
    )(pos, ys, h, top_w, g_ple.reshape(1, d))


def _routing_plan(top_i, n_experts, tm, n_tiles):
    eid = top_i[:, :TOP_K].reshape(-1)
    a = eid.shape[0]
    onehot = (eid[:, None] == jnp.arange(n_experts, dtype=jnp.int32)[None, :]).astype(jnp.int32)
    csum = jnp.cumsum(onehot, axis=0)
    counts = csum[-1]
    rank = jnp.sum(onehot * csum, axis=1) - 1
    tiles_per = (counts + tm - 1) // tm
    cum_tiles = jnp.cumsum(tiles_per)
    pad_off = (cum_tiles - tiles_per) * tm
    pos = jnp.sum(onehot * pad_off[None, :], axis=1) + rank
    row_token = jnp.zeros((n_tiles * tm,), jnp.int32).at[pos].set(
        jnp.arange(a, dtype=jnp.int32) // TOP_K)
    n_valid = cum_tiles[-1]
    tile_ids = jnp.arange(n_tiles, dtype=jnp.int32)
    te = jnp.sum((tile_ids[:, None] >= cum_tiles[None, :]).astype(jnp.int32), axis=1)
    last_live = jnp.sum((n_valid - 1 >= cum_tiles).astype(jnp.int32))
    te = jnp.where(tile_ids < n_valid, te, last_live)
    te = jnp.minimum(te, n_experts - 1)
    return row_token, pos.astype(jnp.int32), te.astype(jnp.int32), n_valid.reshape(1).astype(jnp.int32)


def _rope_tables(pos, dk):
    half = dk // 2
    inv = jnp.power(ROPE_THETA, -jnp.arange(half, dtype=F32) * 2.0 / dk)
    ang = pos.astype(F32)[:, None] * inv[None, :]
    cos = jnp.cos(ang)
    sin = jnp.sin(ang)
    reps = LANES // dk
    cos_t = jnp.tile(jnp.concatenate([cos, cos], axis=1), (1, reps))
    sin_t = jnp.tile(jnp.concatenate([-sin, sin], axis=1), (1, reps))
    return cos_t, sin_t


def _layer(x, pe16, p, cache_k, cache_v, layer, page_table, dims, lam_init):
    b, s, db, t_new, n_heads, dk = dims
    t, d = x.shape
    tp = b * s
    dv = 2 * dk
    qk_w = 2 * n_heads * dk
    att_w = n_heads * dv
    gm_w = p["w_gmlp_out"].shape[0]
    n_experts = p["w_router"].shape[1]
    f = p["w_down"].shape[1]
    assert qk_w == att_w == gm_w, "segment widths must match the column block"
    tn = qk_w
    tm = _pick_tile(t, (768, 512, 384, 256, 128))

    w_in = p["w_in"].astype(BF16)
    xn = _rmsnorm(x, p["g_mix"], tm)

    g_qk = jnp.concatenate([jnp.tile(p["g_q"], qk_w // dk), jnp.tile(p["g_k"], qk_w // dk)]
                           ).reshape(1, 2 * qk_w)
    scale = jnp.concatenate([jnp.full((qk_w,), dk ** -0.5, F32), jnp.ones((qk_w,), F32)]
                            ).reshape(1, 2 * qk_w)
    lane = jnp.arange(LANES)
    blk = (lane[:, None] // dk == lane[None, :] // dk).astype(BF16)
    qk32, qk16 = _mm(xn, w_in, functools.partial(_ep_qk, dk=dk), tm=tm, tn=tn,
                     n_col_blocks=2, w_col0=0,
                     extras=[(g_qk, "col"), (scale, "col"), (p["cos"], "row"),
                             (p["sin"], "row"), (blk, "full")],
                     out_dtypes=[F32, BF16])
    v32, v16 = _mm(xn, w_in, _ep_copy2, tm=tm, tn=tn, n_col_blocks=1, w_col0=2,
                   out_dtypes=[F32, BF16])
    (u16,) = _mm(xn, w_in, _ep_gelu, tm=tm, tn=tn, n_col_blocks=1, w_col0=3,
                 out_dtypes=[BF16])
    (gv,) = _mm(xn, w_in, _ep_gelu_ln, tm=tm, tn=tn, n_col_blocks=1, w_col0=4,
                extras=[(p["g_ln_v"].reshape(1, gm_w), "col"),
                        (p["b_ln_v"].reshape(1, gm_w), "col")],
                out_dtypes=[F32])
    (gates16,) = _mm(xn, w_in, _ep_sigmoid, tm=tm, tn=tn, n_col_blocks=2 * d // tn, w_col0=5,
                     out_dtypes=[BF16])

    lam_vec = jnp.zeros((SUBLANES, LANES), F32).at[:4, :dk].set(
        jnp.stack([p["lam_q1"], p["lam_k1"], p["lam_q2"], p["lam_k2"]]))
    att_p = _prompt_attention(qk16, v16, lam_vec, p["g_subln"], b=b, s=s, h=n_heads,
                              dk=dk, lam_init=lam_init)
    n_sub = 2 * n_heads
    q_s = qk16[tp:, :qk_w].reshape(db, t_new, n_sub, dk)
    qbd = jnp.einsum("btgd,gk->bgtkd", q_s, jnp.eye(n_sub, dtype=BF16)
                     ).reshape(db, n_sub * t_new, qk_w)
    k_new = qk32[tp:, qk_w:].reshape(db, t_new, n_sub * dk)
    k_new_t = jnp.pad(jnp.transpose(k_new, (0, 2, 1)), ((0, 0), (0, 0), (0, LANES - t_new)))
    v_new = jnp.pad(jnp.transpose(v32[tp:].reshape(db, t_new, n_heads, dv), (0, 2, 1, 3)),
                    ((0, 0), (0, 0), (0, LANES - t_new), (0, 0)))
    att_s = _sample_attention(qbd, k_new_t, v_new, cache_k, cache_v, layer, page_table, lam_vec,
                              p["g_subln"], n_heads=n_heads, t_new=t_new, lam_init=lam_init)
    att16 = jnp.concatenate([att_p, att_s.astype(BF16)], axis=0)

    gm16 = _gmlp(u16, gv, p["wmix"], p["bmix"], n_prompt_chunks=tp // LANES)

    merged = _merge(att16, gm16, p["w_att_out"].astype(BF16), p["w_gmlp_out"].astype(BF16),
                    gates16, tm=tm, tn=tn)
    (h,) = _mm(merged, p["w_o"].astype(BF16), _ep_residual, tm=tm, tn=tn,
               n_col_blocks=d // tn, extras=[(x, "tile")], out_dtypes=[F32])

    wr = jnp.zeros((d, LANES), F32).at[:, :n_experts].set(p["w_router"])
    wr_hi = wr.astype(BF16)
    wr_lo = (wr - wr_hi.astype(F32)).astype(BF16)
    b_pad = jnp.zeros((1, LANES), F32).at[0, :n_experts].set(p["b_router"])
    tmr = _pick_tile(t, (256, 128))
    hn_slabs, top_i, top_w = _router(h, p["g_ffn"], wr_hi, wr_lo, b_pad, n_experts=n_experts,
                                     tm=tmr)
    tme = 256 if (t * TOP_K) % 256 == 0 else 128
    n_tiles = (t * TOP_K) // tme + n_experts
    row_token, pos, te, n_valid = _routing_plan(top_i, n_experts, tme, n_tiles)
    xs = _gather_rows(hn_slabs, row_token, n_valid, n_tiles=n_tiles, tm=tme, spt=d // LANES)
    act = _moe_up(xs, p["w_gate_up"], p["b_gate_up"], te, n_valid,
                  n_tiles=n_tiles, tm=tme, fc=min(1024, f))
    ys = _moe_down(act, p["w_down"], p["b_down"], te, n_valid, n_tiles=n_tiles, tm=tme,
                   tn=min(2048, d))
    h2, h2n = _combine(ys, pos, h, top_w, p["g_ple"], tc=LANES)

    (out,) = _mm(h2n, p["w_ple_gate"].astype(BF16), _ep_ple, tm=tm, tn=tn, n_col_blocks=d // tn,
                 extras=[(h2, "tile"), (pe16, "row"), (p["w_ple_proj"].astype(BF16), "colmat")],
                 out_dtypes=[F32])
    k32 = qk32[:, qk_w:]
    return out, k32, v32, gv


def kernel(x_prompt, x_sample, cache_k, cache_v, page_table, p_prompt, p_sample, g_mix, w_in, g_q, g_k, lam_q1, lam_k1, lam_q2, lam_k2, g_subln, g_ln_v, b_ln_v, w_spatial, b_spatial, w_att_out, w_gmlp_out, w_o, g_ffn, w_router, b_router, w_gate_up, b_gate_up, w_down, b_down, g_ple, w_ple_gate, w_ple_proj):
    b, s, d = x_prompt.shape
    db, t_new, _ = x_sample.shape
    depth = w_in.shape[0]
    page = cache_k.shape[2]
    n_heads, dv = cache_v.shape[3], cache_v.shape[4]
    dk = cache_k.shape[4]
    n_pages = page_table.shape[1]
    tp, ts = b * s, db * t_new
    groups, chunk = w_spatial.shape[1], w_spatial.shape[2]
    assert chunk == LANES and dv == LANES and ts % LANES == 0 and s % LANES == 0

    pos_all = jnp.concatenate([jnp.tile(jnp.arange(s), b),
                               jnp.tile(n_pages * page + jnp.arange(t_new), db)])
    cos_t, sin_t = _rope_tables(pos_all, dk)
    x = jnp.concatenate([x_prompt.reshape(tp, d), x_sample.reshape(ts, d)], axis=0)
    tril = jnp.tril(jnp.ones((chunk, chunk), F32))
    seq_per_chunk = chunk // t_new
    n_pool = cache_k.shape[1]
    cache_kt = jnp.transpose(cache_k, (0, 1, 3, 4, 2)).reshape(depth, n_pool, 2 * n_heads * dk, page)
    cache_v2 = cache_v.reshape(depth, n_pool, page * n_heads, dv)

    kp, vp, ks, vs, gs = [], [], [], [], []
    for i in range(depth):
        lam_init = 0.8 - 0.6 * math.exp(-0.3 * i)
        ws = w_spatial[i] * tril
        ws_s = jnp.einsum("ab,gts->gatbs", jnp.eye(seq_per_chunk, dtype=F32),
                          ws[:, :t_new, :t_new]).reshape(groups, chunk, chunk)
        wmix = jnp.stack([ws, ws_s]).astype(BF16)
        b_p = b_spatial[i]
        b_s = jnp.tile(b_spatial[i][:, :t_new], (1, seq_per_chunk))
        bmix = jnp.broadcast_to(jnp.stack([b_p, b_s])[..., None],
                                (2, groups, chunk, LANES)).astype(F32)
        p = dict(g_mix=g_mix[i], w_in=w_in[i], g_q=g_q[i], g_k=g_k[i], g_subln=g_subln[i],
                 g_ln_v=g_ln_v[i], b_ln_v=b_ln_v[i], w_att_out=w_att_out[i],
                 w_gmlp_out=w_gmlp_out[i], w_o=w_o[i], g_ffn=g_ffn[i], w_router=w_router[i],
                 b_router=b_router[i], w_gate_up=w_gate_up[i], b_gate_up=b_gate_up[i],
                 w_down=w_down[i], b_down=b_down[i], g_ple=g_ple[i], w_ple_gate=w_ple_gate[i],
                 w_ple_proj=w_ple_proj[i], lam_q1=lam_q1[i], lam_k1=lam_k1[i],
                 lam_q2=lam_q2[i], lam_k2=lam_k2[i], cos=cos_t, sin=sin_t,
                 wmix=wmix, bmix=bmix)
        pe16 = jnp.concatenate([p_prompt[i].reshape(tp, -1), p_sample[i].reshape(ts, -1)],
                               axis=0).astype(BF16)
        x, k32, v32, gv = _layer(x, pe16, p, cache_kt, cache_v2, i, page_table,
                                 (b, s, db, t_new, n_heads, dk), lam_init)
        kp.append(k32[:tp].reshape(b, s, 2 * n_heads, dk))
        vp.append(v32[:tp].reshape(b, s, n_heads, dv))
        ks.append(k32[tp:].reshape(db, t_new, 2 * n_heads, dk))
        vs.append(v32[tp:].reshape(db, t_new, n_heads, dv))
        gs.append(gv[tp:].reshape(db, t_new, -1))
    return (x[:tp].reshape(b, s, d), x[tp:].reshape(db, t_new, d),
            jnp.stack(kp), jnp.stack(vp), jnp.stack(ks), jnp.stack(vs), jnp.stack(gs))
```

```python
import functools
import math

import jax
import jax.numpy as jnp
from jax import lax
from jax.experimental import pallas as pl
from jax.experimental.pallas import tpu as pltpu

EPS = 1e-6
NEG_INF = -1e30
MASKED = -3.0e38
ROPE_THETA = 10000.0
SWIGLU_ALPHA = 1.702
SWIGLU_LIMIT = 7.0
TOP_K = 4
LANES = 128
SUBLANES = 8
VMEM_LIMIT_BYTES = 56 * 1024 * 1024
DMA_UNROLL = 8
AHEAD = 1
BF16 = jnp.bfloat16
F32 = jnp.float32


def _pick_tile(n, candidates):
    for c in candidates:
        if n % c == 0:
            return c
    raise ValueError(f"no tile in {candidates} divides {n}")


def _params(n_axes):
    return pltpu.CompilerParams(
        dimension_semantics=("arbitrary",) * n_axes,
        vmem_limit_bytes=VMEM_LIMIT_BYTES)


def _gelu(x):
    c = math.sqrt(2.0 / math.pi)
    return x * (0.5 * (1.0 + jnp.tanh(c * (x + 0.044715 * (x * x * x)))))


def _sigmoid(x):
    return 1.0 / (1.0 + jnp.exp(-x))


def _rmsnorm_kernel(x_ref, g_ref, o_ref):
    x = x_ref[...]
    y = x * lax.rsqrt(jnp.mean(x * x, axis=-1, keepdims=True) + EPS) * g_ref[...]
    o_ref[...] = y.astype(o_ref.dtype)


def _rmsnorm(x, g, tm):
    m, d = x.shape
    return pl.pallas_call(
        _rmsnorm_kernel,
        grid=(m // tm,),
        in_specs=[pl.BlockSpec((tm, d), lambda i: (i, 0)),
                  pl.BlockSpec((1, d), lambda i: (0, 0))],
        out_specs=pl.BlockSpec((tm, d), lambda i: (i, 0)),
        out_shape=jax.ShapeDtypeStruct((m, d), BF16),
        compiler_params=_params(1),
    )(x, g.reshape(1, d))


def _mm_kernel(*refs, epilogue, n_extra):
    x_ref, w_ref = refs[0], refs[1]
    extra = refs[2:2 + n_extra]
    outs = refs[2 + n_extra:]
    acc = jnp.dot(x_ref[...], w_ref[...], preferred_element_type=F32)
    epilogue(acc, extra, outs)


def _mm(x, w, epilogue, *, tm, tn, n_col_blocks, w_col0=0, extras=(), out_dtypes):
    m, k = x.shape
    nj, ni = n_col_blocks, m // tm
    in_specs = [pl.BlockSpec((tm, k), lambda j, i: (i, 0)),
                pl.BlockSpec((k, tn), lambda j, i: (0, w_col0 + j))]
    args = [x, w]
    for arr, kind in extras:
        if kind == "tile":
            spec = pl.BlockSpec((tm, tn), lambda j, i: (i, j))
        elif kind == "tile2":
            spec = pl.BlockSpec((tm, tn), lambda j, i: (i, nj + j))
        elif kind == "row":
            spec = pl.BlockSpec((tm, arr.shape[1]), lambda j, i: (i, 0))
        elif kind == "col":
            spec = pl.BlockSpec((1, tn), lambda j, i: (0, j))
        elif kind == "colmat":
            spec = pl.BlockSpec((arr.shape[0], tn), lambda j, i: (0, j))
        elif kind == "full":
            spec = pl.BlockSpec(arr.shape, lambda j, i, nd=arr.ndim: (0,) * nd)
        else:
            raise ValueError(kind)
        in_specs.append(spec)
        args.append(arr)
    out_specs = [pl.BlockSpec((tm, tn), lambda j, i: (i, j)) for _ in out_dtypes]
    out_shape = [jax.ShapeDtypeStruct((m, nj * tn), dt) for dt in out_dtypes]
    res = pl.pallas_call(
        functools.partial(_mm_kernel, epilogue=epilogue, n_extra=len(extras)),
        grid=(nj, ni),
        in_specs=in_specs,
        out_specs=out_specs,
        out_shape=out_shape,
        compiler_params=_params(2),
    )(*args)
    return res


def _ep_qk(acc, extra, outs, *, dk):
    g_ref, scale_ref, cos_ref, sin_ref, blk_ref = extra
    o32, o16 = outs
    cos = cos_ref[...]
    sin = sin_ref[...]
    blk = blk_ref[...]
    half = dk // 2
    lane = lax.broadcasted_iota(jnp.int32, (acc.shape[0], LANES), 1)
    upper = (lane & half) != 0
    for cb in range(acc.shape[1] // LANES):
        sl = slice(cb * LANES, (cb + 1) * LANES)
        z = acc[:, sl]
        ss = jnp.dot((z * z).astype(BF16), blk, preferred_element_type=F32)
        y = z * lax.rsqrt(ss * (1.0 / dk) + EPS) * g_ref[:, sl]
        partner = jnp.where(upper, pltpu.roll(y, half, 1), pltpu.roll(y, LANES - half, 1))
        r = y * cos + partner * sin
        o32[:, sl] = r
        o16[:, sl] = (r * scale_ref[:, sl]).astype(BF16)


def _ep_copy2(acc, extra, outs):
    outs[0][...] = acc
    outs[1][...] = acc.astype(BF16)


def _ep_gelu(acc, extra, outs):
    outs[0][...] = _gelu(acc).astype(outs[0].dtype)


def _ep_gelu_ln(acc, extra, outs):
    g_ref, b_ref = extra
    y = _gelu(acc)
    yc = y - jnp.mean(y, axis=-1, keepdims=True)
    z = yc * lax.rsqrt(jnp.mean(yc * yc, axis=-1, keepdims=True) + EPS)
    outs[0][...] = z * g_ref[...] + b_ref[...]


def _ep_sigmoid(acc, extra, outs):
    outs[0][...] = _sigmoid(acc).astype(outs[0].dtype)


def _ep_residual(acc, extra, outs):
    outs[0][...] = extra[0][...] + acc


def _ep_ple(acc, extra, outs):
    h_ref, pe_ref, wp_ref = extra
    proj = jnp.dot(pe_ref[...], wp_ref[...], preferred_element_type=F32)
    outs[0][...] = h_ref[...] + _sigmoid(acc) * proj


def _diff_lambda(lam_ref, lam_init):
    v = lam_ref[...]
    a = jnp.sum(v[0:1, :] * v[1:2, :], axis=-1, keepdims=True)
    b = jnp.sum(v[2:3, :] * v[3:4, :], axis=-1, keepdims=True)
    return jnp.exp(a) - jnp.exp(b) + lam_init


def _subln(o, g, lam_init):
    y = o * lax.rsqrt(jnp.mean(o * o, axis=-1, keepdims=True) + EPS) * g
    return y * (1.0 - lam_init)


def _prompt_attn_kernel(q_ref, k_ref, v_ref, lam_ref, g_ref, o_ref,
                        m_sc, l_sc, acc_sc, *, tq, tk, rg, dk, lam_init):
    qi = pl.program_id(2)
    q0 = qi * tq
    q = q_ref[...]
    lane = lax.broadcasted_iota(jnp.int32, q.shape, 1)
    zero = jnp.zeros_like(q)
    q2 = jnp.concatenate([jnp.where(lane < dk, q, zero),
                          jnp.where(lane >= dk, q, zero)], axis=0)
    m_sc[...] = jnp.full(m_sc.shape, NEG_INF, F32)
    l_sc[...] = jnp.zeros(l_sc.shape, F32)
    acc_sc[...] = jnp.zeros(acc_sc.shape, F32)

    def step(j, masked):
        start = pl.multiple_of(j * tk, tk)
        kj = k_ref[pl.ds(start, tk), :]
        vj = v_ref[pl.ds(start, tk), :]
        m_all, l_all, acc_all = m_sc[...], l_sc[...], acc_sc[...]
        m_out, l_out, acc_out = [], [], []
        n_groups = 2 * tq // rg

        def scores(r):
            return lax.dot_general(q2[r * rg:(r + 1) * rg], kj, (((1,), (1,)), ((), ())),
                                   preferred_element_type=F32)

        pending = [scores(r) for r in range(min(AHEAD, n_groups))]
        for r in range(n_groups):
            rows = slice(r * rg, (r + 1) * rg)
            s = pending.pop(0)
            if r + AHEAD < n_groups:
                pending.append(scores(r + AHEAD))
            if masked:
                row = lax.broadcasted_iota(jnp.int32, (rg, tk), 0) + (r * rg) % tq
                col = lax.broadcasted_iota(jnp.int32, (rg, tk), 1)
                s = jnp.where(col - row <= q0 - start, s, NEG_INF)
            m_old = m_all[rows]
            m_new = jnp.maximum(m_old, jnp.max(s, axis=-1, keepdims=True))
            alpha = jnp.exp(m_old - m_new)
            p = jnp.exp(s - m_new)
            l_out.append(alpha * l_all[rows] + jnp.sum(p, axis=-1, keepdims=True))
            acc_out.append(alpha * acc_all[rows] + jnp.dot(
                p.astype(BF16), vj, preferred_element_type=F32))
            m_out.append(m_new)
        m_sc[...] = jnp.concatenate(m_out, axis=0)
        l_sc[...] = jnp.concatenate(l_out, axis=0)
        acc_sc[...] = jnp.concatenate(acc_out, axis=0)

    def body(j, carry):
        step(j, False)
        return carry

    n_full = q0 // tk
    lax.fori_loop(0, n_full, body, 0)
    step(n_full, True)

    lam = _diff_lambda(lam_ref, lam_init)
    o = acc_sc[...] / l_sc[...]
    od = o[:tq] - lam * o[tq:]
    o_ref[...] = _subln(od, g_ref[...], lam_init).astype(o_ref.dtype)


def _prompt_attention(qk16, v16, lam_vec, g_subln, *, b, s, h, dk, lam_init):
    dv = 2 * dk
    assert dv == LANES
    tq = _pick_tile(s, (256, 128))
    tk = _pick_tile(s, (1024, 512, 256, 128))
    assert tk % tq == 0
    nq = s // tq
    return pl.pallas_call(
        functools.partial(_prompt_attn_kernel, tq=tq, tk=tk, rg=min(256, tq), dk=dk,
                          lam_init=lam_init),
        grid=(b, h, nq),
        in_specs=[pl.BlockSpec((tq, LANES), lambda bi, hi, qi: (bi * nq + qi, hi)),
                  pl.BlockSpec((s, LANES), lambda bi, hi, qi: (bi, h + hi)),
                  pl.BlockSpec((s, LANES), lambda bi, hi, qi: (bi, hi)),
                  pl.BlockSpec((SUBLANES, LANES), lambda bi, hi, qi: (0, 0)),
                  pl.BlockSpec((1, LANES), lambda bi, hi, qi: (0, 0))],
        out_specs=pl.BlockSpec((tq, LANES), lambda bi, hi, qi: (bi * nq + qi, hi)),
        out_shape=jax.ShapeDtypeStruct((b * s, h * dv), BF16),
        scratch_shapes=[pltpu.VMEM((2 * tq, 1), F32),
                        pltpu.VMEM((2 * tq, 1), F32),
                        pltpu.VMEM((2 * tq, LANES), F32)],
        compiler_params=_params(3),
    )(qk16, qk16, v16, lam_vec, g_subln.reshape(1, dv))


def _sample_attn_kernel(pt_ref, qbd_ref, knew_ref, vnew_ref, lam_ref, g_ref, *rest,
                        pp, page, n_heads, t_new, lam_init):
    kpages = rest[:pp]
    vpages = rest[pp:2 * pp]
    o_ref = rest[2 * pp]
    kb_sc, vb_sc, m_sc, l_sc, acc_sc = rest[2 * pp + 1:]
    step = pl.program_id(1)
    n_steps = pl.num_programs(1)
    dv = LANES
    rph = 2 * t_new

    @pl.when(step == 0)
    def _():
        m_sc[...] = jnp.full(m_sc.shape, NEG_INF, F32)
        l_sc[...] = jnp.zeros(l_sc.shape, F32)
        acc_sc[...] = jnp.zeros(acc_sc.shape, F32)

    qbd = qbd_ref[...]

    def update(s, v_of_head):
        m_old = m_sc[...]
        m_new = jnp.maximum(m_old, jnp.max(s, axis=-1, keepdims=True))
        alpha = jnp.exp(m_old - m_new)
        p = jnp.exp(s - m_new)
        l_sc[...] = alpha * l_sc[...] + jnp.sum(p, axis=-1, keepdims=True)
        pb = p.astype(BF16)
        for hh in range(n_heads):
            rows = pl.ds(hh * rph, rph)
            acc_sc[rows, :] = alpha[hh * rph:(hh + 1) * rph] * acc_sc[rows, :] + jnp.dot(
                pb[hh * rph:(hh + 1) * rph], v_of_head(hh), preferred_element_type=F32)
        m_sc[...] = m_new

    for p_i in range(pp):
        kb_sc[:, pl.ds(p_i * page, page)] = kpages[p_i][...].astype(BF16)
        for hh in range(n_heads):
            vb_sc[hh, pl.ds(p_i * page, page), :] = (
                vpages[p_i][pl.ds(hh, page, stride=n_heads), :].astype(BF16))
    s = jnp.dot(qbd, kb_sc[...], preferred_element_type=F32)
    update(s, lambda hh: vb_sc[hh])

    @pl.when(step == n_steps - 1)
    def _():
        sn = jnp.dot(qbd, knew_ref[...].astype(BF16), preferred_element_type=F32)
        row = lax.broadcasted_iota(jnp.int32, sn.shape, 0)
        col = lax.broadcasted_iota(jnp.int32, sn.shape, 1)
        keep = col <= (row & (t_new - 1))
        update(jnp.where(keep, sn, NEG_INF), lambda hh: vnew_ref[hh].astype(BF16))

        lam = _diff_lambda(lam_ref, lam_init)
        o = acc_sc[...] / l_sc[...]
        for hh in range(n_heads):
            r1 = hh * rph
            od = o[r1:r1 + t_new] - lam * o[r1 + t_new:r1 + rph]
            o_ref[:, hh * dv:(hh + 1) * dv] = _subln(od, g_ref[...], lam_init)


def _sample_attention(qbd, k_new_t, v_new, cache_kt, cache_v2, layer, page_table, lam_vec,
                      g_subln, *, n_heads, t_new, lam_init):
    db, rows, width = qbd.shape
    page = cache_kt.shape[3]
    n_pages = page_table.shape[1]
    pp = _pick_tile(n_pages, (8, 4, 2, 1))
    n_steps = n_pages // pp
    dv = LANES
    assert page == LANES and t_new & (t_new - 1) == 0 and t_new % SUBLANES == 0

    def kspec(p_i):
        return pl.BlockSpec((None, None, width, page),
                            lambda b, st, pt: (layer, pt[b, st * pp + p_i], 0, 0))

    def vspec(p_i):
        return pl.BlockSpec((None, None, page * n_heads, dv),
                            lambda b, st, pt: (layer, pt[b, st * pp + p_i], 0, 0))

    in_specs = [pl.BlockSpec((None, rows, width), lambda b, st, pt: (b, 0, 0)),
                pl.BlockSpec((None, width, LANES), lambda b, st, pt: (b, 0, 0)),
                pl.BlockSpec((None, n_heads, LANES, dv), lambda b, st, pt: (b, 0, 0, 0)),
                pl.BlockSpec((SUBLANES, LANES), lambda b, st, pt: (0, 0)),
                pl.BlockSpec((1, dv), lambda b, st, pt: (0, 0))]
    in_specs += [kspec(p_i) for p_i in range(pp)]
    in_specs += [vspec(p_i) for p_i in range(pp)]
    grid_spec = pltpu.PrefetchScalarGridSpec(
        num_scalar_prefetch=1,
        grid=(db, n_steps),
        in_specs=in_specs,
        out_specs=pl.BlockSpec((t_new, n_heads * dv), lambda b, st, pt: (b, 0)),
        scratch_shapes=[pltpu.VMEM((width, pp * page), BF16),
                        pltpu.VMEM((n_heads, pp * page, dv), BF16),
                        pltpu.VMEM((rows, 1), F32),
                        pltpu.VMEM((rows, 1), F32),
                        pltpu.VMEM((rows, dv), F32)])
    return pl.pallas_call(
        functools.partial(_sample_attn_kernel, pp=pp, page=page, n_heads=n_heads,
                          t_new=t_new, lam_init=lam_init),
        grid_spec=grid_spec,
        out_shape=jax.ShapeDtypeStruct((db * t_new, n_heads * dv), F32),
        compiler_params=_params(2),
    )(page_table, qbd, k_new_t, v_new, lam_vec, g_subln.reshape(1, dv),
      *([cache_kt] * pp), *([cache_v2] * pp))


def _gmlp_kernel(u_ref, gv_ref, w_ref, b_ref, o_ref, *, groups):
    for g in range(groups):
        sl = slice(g * LANES, (g + 1) * LANES)
        s = jnp.dot(w_ref[g], gv_ref[:, sl].astype(BF16), preferred_element_type=F32)
        s = s + b_ref[g]
        o_ref[:, sl] = (u_ref[:, sl].astype(F32) * s).astype(o_ref.dtype)


def _gmlp(u16, gv, wmix, bmix, *, n_prompt_chunks):
    t, width = gv.shape
    groups, chunk = wmix.shape[1], wmix.shape[2]
    assert chunk == LANES and width == groups * LANES
    sel = lambda c: (jnp.where(c < n_prompt_chunks, 0, 1), 0, 0, 0)
    return pl.pallas_call(
        functools.partial(_gmlp_kernel, groups=groups),
        grid=(t // chunk,),
        in_specs=[pl.BlockSpec((chunk, width), lambda c: (c, 0)),
                  pl.BlockSpec((chunk, width), lambda c: (c, 0)),
                  pl.BlockSpec((None, groups, chunk, chunk), sel),
                  pl.BlockSpec((None, groups, chunk, LANES), sel)],
        out_specs=pl.BlockSpec((chunk, width), lambda c: (c, 0)),
        out_shape=jax.ShapeDtypeStruct((t, width), BF16),
        compiler_params=_params(1),
    )(u16, gv, wmix, bmix)


def _merge_kernel(att_ref, gm_ref, wa_ref, wg_ref, ga_ref, gg_ref, o_ref):
    a = jnp.dot(att_ref[...], wa_ref[...], preferred_element_type=F32)
    g = jnp.dot(gm_ref[...], wg_ref[...], preferred_element_type=F32)
    o_ref[...] = (ga_ref[...].astype(F32) * a + gg_ref[...].astype(F32) * g).astype(o_ref.dtype)


def _merge(att16, gm16, wa, wg, gates16, *, tm, tn):
    t, ka = att16.shape
    kg = gm16.shape[1]
    d = wa.shape[1]
    nj = d // tn
    return pl.pallas_call(
        _merge_kernel,
        grid=(nj, t // tm),
        in_specs=[pl.BlockSpec((tm, ka), lambda j, i: (i, 0)),
                  pl.BlockSpec((tm, kg), lambda j, i: (i, 0)),
                  pl.BlockSpec((ka, tn), lambda j, i: (0, j)),
                  pl.BlockSpec((kg, tn), lambda j, i: (0, j)),
                  pl.BlockSpec((tm, tn), lambda j, i: (i, j)),
                  pl.BlockSpec((tm, tn), lambda j, i: (i, nj + j))],
        out_specs=pl.BlockSpec((tm, tn), lambda j, i: (i, j)),
        out_shape=jax.ShapeDtypeStruct((t, d), BF16),
        compiler_params=_params(2),
    )(att16, gm16, wa, wg, gates16, gates16)


def _router_kernel(h_ref, g_ref, wh_ref, wl_ref, b_ref, xn_ref, ti_ref, tw_ref,
                   *, n_experts):
    x = h_ref[...]
    xn = x * lax.rsqrt(jnp.mean(x * x, axis=-1, keepdims=True) + EPS) * g_ref[...]
    tm, d = xn.shape
    xh = xn.astype(BF16)
    xl = (xn - xh.astype(F32)).astype(BF16)
    wh = wh_ref[...]
    logits = (jnp.dot(xh, wh, preferred_element_type=F32)
              + jnp.dot(xl, wh, preferred_element_type=F32)
              + jnp.dot(xh, wl_ref[...], preferred_element_type=F32)) + b_ref[...]
    lane = lax.broadcasted_iota(jnp.int32, logits.shape, 1)
    lane_f = lane.astype(F32)
    cur = jnp.where(lane < n_experts, logits, MASKED)
    vals, idxs = [], []
    for _ in range(TOP_K):
        m = jnp.max(cur, axis=-1, keepdims=True)
        idx = jnp.min(jnp.where(cur == m, lane_f, float(LANES)), axis=-1, keepdims=True)
        vals.append(m)
        idxs.append(idx)
        cur = jnp.where(lane_f == idx, MASKED, cur)
    es = [jnp.exp(v - vals[0]) for v in vals]
    tot = es[0]
    for e in es[1:]:
        tot = tot + e
    ti = jnp.zeros(logits.shape, F32)
    tw = jnp.zeros(logits.shape, F32)
    for kk in range(TOP_K):
        ti = jnp.where(lane == kk, idxs[kk], ti)
        tw = jnp.where(lane == kk, es[kk] / tot, tw)
    ti_ref[...] = ti.astype(jnp.int32)
    tw_ref[...] = tw
    spt = d // LANES
    for s in range(spt):
        xn_ref[pl.ds(s, tm, stride=spt), :] = xn[:, s * LANES:(s + 1) * LANES]


def _router(h, g_ffn, wr_hi, wr_lo, b_pad, *, n_experts, tm):
    t, d = h.shape
    spt = d // LANES
    return pl.pallas_call(
        functools.partial(_router_kernel, n_experts=n_experts),
        grid=(t // tm,),
        in_specs=[pl.BlockSpec((tm, d), lambda i: (i, 0)),
                  pl.BlockSpec((1, d), lambda i: (0, 0)),
                  pl.BlockSpec((d, LANES), lambda i: (0, 0)),
                  pl.BlockSpec((d, LANES), lambda i: (0, 0)),
                  pl.BlockSpec((1, LANES), lambda i: (0, 0))],
        out_specs=[pl.BlockSpec((tm * spt, LANES), lambda i: (i, 0)),
                   pl.BlockSpec((tm, LANES), lambda i: (i, 0)),
                   pl.BlockSpec((tm, LANES), lambda i: (i, 0))],
        out_shape=[jax.ShapeDtypeStruct((t * spt, LANES), F32),
                   jax.ShapeDtypeStruct((t, LANES), jnp.int32),
                   jax.ShapeDtypeStruct((t, LANES), F32)],
        compiler_params=_params(1),
    )(h, g_ffn.reshape(1, d), wr_hi, wr_lo, b_pad)


def _gather_kernel(idx_ref, nv_ref, src_ref, o_ref, buf, sem, *, tm, spt):
    t = pl.program_id(0)
    n_live = nv_ref[0]

    def copy(tile, r, slot):
        tok = idx_ref[tile * tm + r]
        return pltpu.make_async_copy(
            src_ref.at[pl.ds(pl.multiple_of(tok * spt, spt), spt), :],
            buf.at[slot, pl.ds(pl.multiple_of(r * spt, spt), spt), :],
            sem.at[slot])

    def start_tile(tile, slot):
        def body(r, c):
            copy(tile, r, slot).start()
            return c
        lax.fori_loop(0, tm, body, 0, unroll=DMA_UNROLL)

    def wait_tile(tile, slot):
        def body(r, c):
            copy(tile, r, slot).wait()
            return c
        lax.fori_loop(0, tm, body, 0, unroll=DMA_UNROLL)

    @pl.when(jnp.logical_and(t == 0, n_live > 0))
    def _():
        start_tile(0, 0)

    for slot in (0, 1):
        mine = (t % 2) == slot

        @pl.when(jnp.logical_and(mine, t + 1 < n_live))
        def _():
            start_tile(t + 1, 1 - slot)

        @pl.when(jnp.logical_and(mine, t < n_live))
        def _():
            wait_tile(t, slot)
            for s in range(spt):
                o_ref[:, s * LANES:(s + 1) * LANES] = (
                    buf[slot, pl.ds(s, tm, stride=spt), :].astype(o_ref.dtype))

    @pl.when(t >= n_live)
    def _():
        o_ref[...] = jnp.zeros(o_ref.shape, o_ref.dtype)


def _gather_rows(src, idx, n_valid_tiles, *, n_tiles, tm, spt):
    grid_spec = pltpu.PrefetchScalarGridSpec(
        num_scalar_prefetch=2,
        grid=(n_tiles,),
        in_specs=[pl.BlockSpec(memory_space=pl.ANY)],
        out_specs=pl.BlockSpec((tm, spt * LANES), lambda t, idx, nv: (t, 0)),
        scratch_shapes=[pltpu.VMEM((2, tm * spt, LANES), src.dtype),
                        pltpu.SemaphoreType.DMA((2,))])
    return pl.pallas_call(
        functools.partial(_gather_kernel, tm=tm, spt=spt),
        grid_spec=grid_spec,
        out_shape=jax.ShapeDtypeStruct((n_tiles * tm, spt * LANES), BF16),
        compiler_params=_params(1),
    )(idx, n_valid_tiles, src)


def _new_group(te_ref, t):
    prev = te_ref[jnp.maximum(t - 1, 0)]
    return jnp.logical_or(t == 0, te_ref[t] != prev)


def _moe_up_kernel(te_ref, nv_ref, xs_ref, wg_ref, wl_ref, bg_ref, bl_ref, o_ref,
                   wg_sc, wl_sc):
    t = pl.program_id(1)
    valid = t < nv_ref[0]

    @pl.when(jnp.logical_and(valid, _new_group(te_ref, t)))
    def _():
        wg_sc[...] = wg_ref[...].astype(BF16)
        wl_sc[...] = wl_ref[...].astype(BF16)

    @pl.when(valid)
    def _():
        x = xs_ref[...]
        gl =jnp.dot(x, wg_sc[...], preferred_element_type=F32) + bg_ref[...]
        ln = jnp.dot(x, wl_sc[...], preferred_element_type=F32) + bl_ref[...]
        glu = jnp.minimum(gl, SWIGLU_LIMIT)
        lin = jnp.clip(ln, -SWIGLU_LIMIT, SWIGLU_LIMIT)
        act = glu * _sigmoid(SWIGLU_ALPHA * glu) * (lin + 1.0)
        o_ref[...] = act.astype(o_ref.dtype)

    @pl.when(jnp.logical_not(valid))
    def _():
        o_ref[...] = jnp.zeros(o_ref.shape, o_ref.dtype)


def _moe_up(xs, w_gate_up, b_gate_up, tile_expert, n_valid, *, n_tiles, tm, fc):
    e, d, f2 = w_gate_up.shape
    f = f2 // 2
    nc = f // fc
    last = lambda t, nv: jnp.minimum(t, nv[0] - 1)
    grid_spec = pltpu.PrefetchScalarGridSpec(
        num_scalar_prefetch=2,
        grid=(nc, n_tiles),
        in_specs=[pl.BlockSpec((tm, d), lambda c, t, te, nv: (last(t, nv), 0)),
                  pl.BlockSpec((None, d, fc), lambda c, t, te, nv: (te[t], 0, c)),
                  pl.BlockSpec((None, d, fc), lambda c, t, te, nv: (te[t], 0, nc + c)),
                  pl.BlockSpec((None, 1, fc), lambda c, t, te, nv: (te[t], 0, c)),
                  pl.BlockSpec((None, 1, fc), lambda c, t, te, nv: (te[t], 0, nc + c))],
        out_specs=pl.BlockSpec((tm, fc), lambda c, t, te, nv: (t, c)),
        scratch_shapes=[pltpu.VMEM((d, fc), BF16), pltpu.VMEM((d, fc), BF16)])
    return pl.pallas_call(
        _moe_up_kernel,
        grid_spec=grid_spec,
        out_shape=jax.ShapeDtypeStruct((n_tiles * tm, f), BF16),
        compiler_params=_params(2),
    )(tile_expert, n_valid, xs, w_gate_up, w_gate_up,
      b_gate_up.reshape(e, 1, f2), b_gate_up.reshape(e, 1, f2))


def _moe_down_kernel(te_ref, nv_ref, a_ref, w_ref, b_ref, o_ref, w_sc, *, tm, spo):
    t = pl.program_id(1)
    valid = t < nv_ref[0]

    @pl.when(jnp.logical_and(valid, _new_group(te_ref, t)))
    def _():
        w_sc[...] = w_ref[...].astype(BF16)

    @pl.when(valid)
    def _():
        y = jnp.dot(a_ref[...], w_sc[...], preferred_element_type=F32) + b_ref[...]
        for s in range(y.shape[1] // LANES):
            o_ref[pl.ds(s, tm, stride=spo), :] = y[:, s * LANES:(s + 1) * LANES]

    @pl.when(jnp.logical_not(valid))
    def _():
        o_ref[...] = jnp.zeros(o_ref.shape, o_ref.dtype)


def _moe_down(act, w_down, b_down, tile_expert, n_valid, *, n_tiles, tm, tn):
    e, f, d = w_down.shape
    nn = d // tn
    spo = tn // LANES
    last = lambda t, nv: jnp.minimum(t, nv[0] - 1)
    grid_spec = pltpu.PrefetchScalarGridSpec(
        num_scalar_prefetch=2,
        grid=(nn, n_tiles),
        in_specs=[pl.BlockSpec((tm, f), lambda n, t, te, nv: (last(t, nv), 0)),
                  pl.BlockSpec((None, f, tn), lambda n, t, te, nv: (te[t], 0, n)),
                  pl.BlockSpec((None, 1, tn), lambda n, t, te, nv: (te[t], 0, n))],
        out_specs=pl.BlockSpec((None, tm * spo, LANES), lambda n, t, te, nv: (n, t, 0)),
        scratch_shapes=[pltpu.VMEM((f, tn), BF16)])
    return pl.pallas_call(
        functools.partial(_moe_down_kernel, tm=tm, spo=spo),
        grid_spec=grid_spec,
        out_shape=jax.ShapeDtypeStruct((nn, n_tiles * tm * spo, LANES), F32),
        compiler_params=_params(2),
    )(tile_expert, n_valid, act, w_down, b_down.reshape(e, 1, d))


def _combine_kernel(pos_ref, ys_ref, h_ref, tw_ref, g_ref, o_ref, on_ref, buf, sem,
                    *, tc, nn, spo):
    i = pl.program_id(0)
    n_steps = pl.num_programs(0)

    def copy(tile, r, kk, n, slot):
        p = pos_ref[(tile * tc + r) * TOP_K + kk]
        return pltpu.make_async_copy(
            ys_ref.at[n, pl.ds(pl.multiple_of(p * spo, spo), spo), :],
            buf.at[slot, kk * nn + n, pl.ds(pl.multiple_of(r * spo, spo), spo), :],
            sem.at[slot])

    def start_tile(tile, slot):
        def body(r, c):
            for kk in range(TOP_K):
                for n in range(nn):
                    copy(tile, r, kk, n, slot).start()
            return c
        lax.fori_loop(0, tc, body, 0, unroll=DMA_UNROLL // 2)

    def wait_tile(tile, slot):
        def body(r, c):
            for kk in range(TOP_K):
                for n in range(nn):
                    copy(tile, r, kk, n, slot).wait()
            return c
        lax.fori_loop(0, tc, body, 0, unroll=DMA_UNROLL // 2)

    @pl.when(i == 0)
    def _():
        start_tile(0, 0)

    for slot in (0, 1):
        mine = (i % 2) == slot

        @pl.when(jnp.logical_and(mine, i + 1 < n_steps))
        def _():
            start_tile(i + 1, 1 - slot)

        @pl.when(mine)
        def _():
            wait_tile(i, slot)
            tw = tw_ref[...]
            cols = []
            for n in range(nn):
                for s in range(spo):
                    acc = None
                    for kk in range(TOP_K):
                        piece = (buf[slot, kk * nn + n, pl.ds(s, tc, stride=spo), :]
                                 * tw[:, kk:kk + 1])
                        acc = piece if acc is None else acc + piece
                    cols.append(acc)
            y = jnp.concatenate(cols, axis=1)
            h2 = h_ref[...] + y
            o_ref[...] = h2
            hn = h2 * lax.rsqrt(jnp.mean(h2 * h2, axis=-1, keepdims=True) + EPS) * g_ref[...]
            on_ref[...] = hn.astype(on_ref.dtype)


def _combine(ys, pos, h, top_w, g_ple, *, tc):
    t, d = h.shape
    nn, _, _ = ys.shape
    spo = d // nn // LANES
    grid_spec = pltpu.PrefetchScalarGridSpec(
        num_scalar_prefetch=1,
        grid=(t // tc,),
        in_specs=[pl.BlockSpec(memory_space=pl.ANY),
                  pl.BlockSpec((tc, d), lambda i, pos: (i, 0)),
                  pl.BlockSpec((tc, LANES), lambda i, pos: (i, 0)),
                  pl.BlockSpec((1, d), lambda i, pos: (0, 0))],
        out_specs=[pl.BlockSpec((tc, d), lambda i, pos: (i, 0)),
                   pl.BlockSpec((tc, d), lambda i, pos: (i, 0))],
        scratch_shapes=[pltpu.VMEM((2, TOP_K * nn, tc * spo, LANES), F32),
                        pltpu.SemaphoreType.DMA((2,))])
    return pl.pallas_call(
        functools.partial(_combine_kernel, tc=tc, nn=nn, spo=spo),
        grid_spec=grid_spec,
        out_shape=[jax.ShapeDtypeStruct((t, d), F32),
                   jax.ShapeDtypeStruct((t, d), BF16)],
        compiler_params=_params(1),
    )(pos, ys, h, top_w, g_ple.reshape(1, d))


def _routing_plan(top_i, n_experts, tm, n_tiles):
    eid = top_i[:, :TOP_K].reshape(-1)
    a = eid.shape[0]
    onehot = (eid[:, None] == jnp.arange(n_experts, dtype=jnp.int32)[None, :]).astype(jnp.int32)
    csum = jnp.cumsum(onehot, axis=0)
    counts = csum[-1]
    rank = jnp.sum(onehot * csum, axis=1) - 1
    tiles_per = (counts + tm - 1) // tm
    cum_tiles = jnp.cumsum(tiles_per)
    pad_off = (cum_tiles - tiles_per) * tm
    pos = jnp.sum(onehot * pad_off[None, :], axis=1) + rank
    row_token = jnp.zeros((n_tiles * tm,), jnp.int32).at[pos].set(
        jnp.arange(a, dtype=jnp.int32) // TOP_K)
    n_valid = cum_tiles[-1]
    tile_ids = jnp.arange(n_tiles, dtype=jnp.int32)
    te = jnp.sum((tile_ids[:, None] >= cum_tiles[None, :]).astype(jnp.int32), axis=1)
    last_live = jnp.sum((n_valid - 1 >= cum_tiles).astype(jnp.int32))
    te = jnp.where(tile_ids < n_valid, te, last_live)
    te = jnp.minimum(te, n_experts - 1)
    return row_token, pos.astype(jnp.int32), te.astype(jnp.int32), n_valid.reshape(1).astype(jnp.int32)


def _rope_tables(pos, dk):
    half = dk // 2
    inv = jnp.power(ROPE_THETA, -jnp.arange(half, dtype=F32) * 2.0 / dk)
    ang = pos.astype(F32)[:, None] * inv[None, :]
    cos = jnp.cos(ang)
    sin = jnp.sin(ang)
    reps = LANES // dk
    cos_t = jnp.tile(jnp.concatenate([cos, cos], axis=1), (1, reps))
    sin_t = jnp.tile(jnp.concatenate([-sin, sin], axis=1), (1, reps))
    return cos_t, sin_t


def _layer(x, pe16, p, cache_k, cache_v, layer, page_table, dims, lam_init):
    b, s, db, t_new, n_heads, dk = dims
    t, d = x.shape
    tp = b * s
    dv = 2 * dk
    qk_w = 2 * n_heads * dk
    att_w = n_heads * dv
    gm_w = p["w_gmlp_out"].shape[0]
    n_experts = p["w_router"].shape[1]
    f = p["w_down"].shape[1]
    assert qk_w == att_w == gm_w, "segment widths must match the column block"
    tn = qk_w
    tm = _pick_tile(t, (768, 512, 384, 256, 128))

    w_in = p["w_in"].astype(BF16)
    xn = _rmsnorm(x, p["g_mix"], tm)

    g_qk = jnp.concatenate([jnp.tile(p["g_q"], qk_w // dk), jnp.tile(p["g_k"], qk_w // dk)]
                           ).reshape(1, 2 * qk_w)
    scale = jnp.concatenate([jnp.full((qk_w,), dk ** -0.5, F32), jnp.ones((qk_w,), F32)]
                            ).reshape(1, 2 * qk_w)
    lane = jnp.arange(LANES)
    blk = (lane[:, None] // dk == lane[None, :] // dk).astype(BF16)
    qk32, qk16 = _mm(xn, w_in, functools.partial(_ep_qk, dk=dk), tm=tm, tn=tn,
                     n_col_blocks=2, w_col0=0,
                     extras=[(g_qk, "col"), (scale, "col"), (p["cos"], "row"),
                             (p["sin"], "row"), (blk, "full")],
                     out_dtypes=[F32, BF16])
    v32, v16 = _mm(xn, w_in, _ep_copy2, tm=tm, tn=tn, n_col_blocks=1, w_col0=2,
                   out_dtypes=[F32, BF16])
    (u16,) = _mm(xn, w_in, _ep_gelu, tm=tm, tn=tn, n_col_blocks=1, w_col0=3,
                 out_dtypes=[BF16])
    (gv,) = _mm(xn, w_in, _ep_gelu_ln, tm=tm, tn=tn, n_col_blocks=1, w_col0=4,
                extras=[(p["g_ln_v"].reshape(1, gm_w), "col"),
                        (p["b_ln_v"].reshape(1, gm_w), "col")],
                out_dtypes=[F32])
    (gates16,) = _mm(xn, w_in, _ep_sigmoid, tm=tm, tn=tn, n_col_blocks=2 * d // tn, w_col0=5,
                     out_dtypes=[BF16])

    lam_vec = jnp.zeros((SUBLANES, LANES), F32).at[:4, :dk].set(
        jnp.stack([p["lam_q1"], p["lam_k1"], p["lam_q2"], p["lam_k2"]]))
    att_p = _prompt_attention(qk16, v16, lam_vec, p["g_subln"], b=b, s=s, h=n_heads,
                              dk=dk, lam_init=lam_init)
    n_sub = 2 * n_heads
    q_s = qk16[tp:, :qk_w].reshape(db, t_new, n_sub, dk)
    qbd = jnp.einsum("btgd,gk->bgtkd", q_s, jnp.eye(n_sub, dtype=BF16)
                     ).reshape(db, n_sub * t_new, qk_w)
    k_new = qk32[tp:, qk_w:].reshape(db, t_new, n_sub * dk)
    k_new_t = jnp.pad(jnp.transpose(k_new, (0, 2, 1)), ((0, 0), (0, 0), (0, LANES - t_new)))
    v_new = jnp.pad(jnp.transpose(v32[tp:].reshape(db, t_new, n_heads, dv), (0, 2, 1, 3)),
                    ((0, 0), (0, 0), (0, LANES - t_new), (0, 0)))
    att_s = _sample_attention(qbd, k_new_t, v_new, cache_k, cache_v, layer, page_table, lam_vec,
                              p["g_subln"], n_heads=n_heads, t_new=t_new, lam_init=lam_init)
    att16 = jnp.concatenate([att_p, att_s.astype(BF16)], axis=0)

    gm16 = _gmlp(u16, gv, p["wmix"], p["bmix"], n_prompt_chunks=tp // LANES)

    merged = _merge(att16, gm16, p["w_att_out"].astype(BF16), p["w_gmlp_out"].astype(BF16),
                    gates16, tm=tm, tn=tn)
    (h,) = _mm(merged, p["w_o"].astype(BF16), _ep_residual, tm=tm, tn=tn,
               n_col_blocks=d // tn, extras=[(x, "tile")], out_dtypes=[F32])

    wr = jnp.zeros((d, LANES), F32).at[:, :n_experts].set(p["w_router"])
    wr_hi = wr.astype(BF16)
    wr_lo = (wr - wr_hi.astype(F32)).astype(BF16)
    b_pad = jnp.zeros((1, LANES), F32).at[0, :n_experts].set(p["b_router"])
    tmr = _pick_tile(t, (256, 128))
    hn_slabs, top_i, top_w = _router(h, p["g_ffn"], wr_hi, wr_lo, b_pad, n_experts=n_experts,
                                     tm=tmr)
    tme = 256 if (t * TOP_K) % 256 == 0 else 128
    n_tiles = (t * TOP_K) // tme + n_experts
    row_token, pos, te, n_valid = _routing_plan(top_i, n_experts, tme, n_tiles)
    xs = _gather_rows(hn_slabs, row_token, n_valid, n_tiles=n_tiles, tm=tme, spt=d // LANES)
    act = _moe_up(xs, p["w_gate_up"], p["b_gate_up"], te, n_valid,
                  n_tiles=n_tiles, tm=tme, fc=min(1024, f))
    ys = _moe_down(act, p["w_down"], p["b_down"], te, n_valid, n_tiles=n_tiles, tm=tme,
                   tn=min(2048, d))
    h2, h2n = _combine(ys, pos, h, top_w, p["g_ple"], tc=LANES)

    (out,) = _mm(h2n, p["w_ple_gate"].astype(BF16), _ep_ple, tm=tm, tn=tn, n_col_blocks=d // tn,
                 extras=[(h2, "tile"), (pe16, "row"), (p["w_ple_proj"].astype(BF16), "colmat")],
                 out_dtypes=[F32])
    k32 = qk32[:, qk_w:]
    return out, k32, v32, gv


def kernel(x_prompt, x_sample, cache_k, cache_v, page_table, p_prompt, p_sample, g_mix, w_in, g_q, g_k, lam_q1, lam_k1, lam_q2, lam_k2, g_subln, g_ln_v, b_ln_v, w_spatial, b_spatial, w_att_out, w_gmlp_out, w_o, g_ffn, w_router, b_router, w_gate_up, b_gate_up, w_down, b_down, g_ple, w_ple_gate, w_ple_proj):
    b, s, d = x_prompt.shape
    db, t_new, _ = x_sample.shape
    depth = w_in.shape[0]
    page = cache_k.shape[2]
    n_heads, dv = cache_v.shape[3], cache_v.shape[4]
    dk = cache_k.shape[4]
    n_pages = page_table.shape[1]
    tp, ts = b * s, db * t_new
    groups, chunk = w_spatial.shape[1], w_spatial.shape[2]
    assert chunk == LANES and dv == LANES and ts % LANES == 0 and s % LANES == 0

    pos_all = jnp.concatenate([jnp.tile(jnp.arange(s), b),
                               jnp.tile(n_pages * page + jnp.arange(t_new), db)])
    cos_t, sin_t = _rope_tables(pos_all, dk)
    x = jnp.concatenate([x_prompt.reshape(tp, d), x_sample.reshape(ts, d)], axis=0)
    tril = jnp.tril(jnp.ones((chunk, chunk), F32))
    seq_per_chunk = chunk // t_new
    n_pool = cache_k.shape[1]
    cache_kt = jnp.transpose(cache_k, (0, 1, 3, 4, 2)).reshape(depth, n_pool, 2 * n_heads * dk, page)
    cache_v2 = cache_v.reshape(depth, n_pool, page * n_heads, dv)

    kp, vp, ks, vs, gs = [], [], [], [], []
    for i in range(depth):
        lam_init = 0.8 - 0.6 * math.exp(-0.3 * i)
        ws = w_spatial[i] * tril
        ws_s = jnp.einsum("ab,gts->gatbs", jnp.eye(seq_per_chunk, dtype=F32),
                          ws[:, :t_new, :t_new]).reshape(groups, chunk, chunk)
        wmix = jnp.stack([ws, ws_s]).astype(BF16)
        b_p = b_spatial[i]
        b_s = jnp.tile(b_spatial[i][:, :t_new], (1, seq_per_chunk))
        bmix = jnp.broadcast_to(jnp.stack([b_p, b_s])[..., None],
                                (2, groups, chunk, LANES)).astype(F32)
        p = dict(g_mix=g_mix[i], w_in=w_in[i], g_q=g_q[i], g_k=g_k[i], g_subln=g_subln[i],
                 g_ln_v=g_ln_v[i], b_ln_v=b_ln_v[i], w_att_out=w_att_out[i],
                 w_gmlp_out=w_gmlp_out[i], w_o=w_o[i], g_ffn=g_ffn[i], w_router=w_router[i],
                 b_router=b_router[i], w_gate_up=w_gate_up[i], b_gate_up=b_gate_up[i],
                 w_down=w_down[i], b_down=b_down[i], g_ple=g_ple[i], w_ple_gate=w_ple_gate[i],
                 w_ple_proj=w_ple_proj[i], lam_q1=lam_q1[i], lam_k1=lam_k1[i],
                 lam_q2=lam_q2[i], lam_k2=lam_k2[i], cos=cos_t, sin=sin_t,
                 wmix=wmix, bmix=bmix)
        pe16 = jnp.concatenate([p_prompt[i].reshape(tp, -1), p_sample[i].reshape(ts, -1)],
                               axis=0).astype(BF16)
        x, k32, v32, gv = _layer(x, pe16, p, cache_kt, cache_v2, i, page_table,
                                 (b, s, db, t_new, n_heads, dk), lam_init)
        kp.append(k32[:tp].reshape(b, s, 2 * n_heads, dk))
        vp.append(v32[:tp].reshape(b, s, n_heads, dv))
        ks.append(k32[tp:].reshape(db, t_new, 2 * n_heads, dk))
        vs.append(v32[tp:].reshape(db, t_new, n_heads, dv))
        gs.append(gv[tp:].reshape(db, t_new, -1))
    return (x[:tp].reshape(b, s, d), x[tp:].reshape(db, t_new, d),
            jnp.stack(kp), jnp.stack(vp), jnp.stack(ks), jnp.stack(vs), jnp.stack(gs))
```

```python
import functools
import math

import jax
import jax.numpy as jnp
from jax import lax
from jax.experimental import pallas as pl
from jax.experimental.pallas import tpu as pltpu

EPS = 1e-6
NEG_INF = -1e30
MASKED = -3.0e38
ROPE_THETA = 10000.0
SWIGLU_ALPHA = 1.702
SWIGLU_LIMIT = 7.0
TOP_K = 4
LANES = 128
SUBLANES = 8
VMEM_LIMIT_BYTES = 56 * 1024 * 1024
DMA_UNROLL = 8
AHEAD = 1
BF16 = jnp.bfloat16
F32 = jnp.float32


def _pick_tile(n, candidates):
    for c in candidates:
        if n % c == 0:
            return c
    raise ValueError(f"no tile in {candidates} divides {n}")


def _params(n_axes):
    return pltpu.CompilerParams(
        dimension_semantics=("arbitrary",) * n_axes,
        vmem_limit_bytes=VMEM_LIMIT_BYTES)


def _gelu(x):
    c = math.sqrt(2.0 / math.pi)
    return x * (0.5 * (1.0 + jnp.tanh(c * (x + 0.044715 * (x * x * x)))))


def _sigmoid(x):
    return 1.0 / (1.0 + jnp.exp(-x))


def _rmsnorm_kernel(x_ref, g_ref, o_ref):
    x = x_ref[...]
    y = x * lax.rsqrt(jnp.mean(x * x, axis=-1, keepdims=True) + EPS) * g_ref[...]
    o_ref[...] = y.astype(o_ref.dtype)


def _rmsnorm(x, g, tm):
    m, d = x.shape
    return pl.pallas_call(
        _rmsnorm_kernel,
        grid=(m // tm,),
        in_specs=[pl.BlockSpec((tm, d), lambda i: (i, 0)),
                  pl.BlockSpec((1, d), lambda i: (0, 0))],
        out_specs=pl.BlockSpec((tm, d), lambda i: (i, 0)),
        out_shape=jax.ShapeDtypeStruct((m, d), BF16),
        compiler_params=_params(1),
    )(x, g.reshape(1, d))


def _mm_kernel(*refs, epilogue, n_extra):
    x_ref, w_ref = refs[0], refs[1]
    extra = refs[2:2 + n_extra]
    outs = refs[2 + n_extra:]
    acc = jnp.dot(x_ref[...], w_ref[...], preferred_element_type=F32)
    epilogue(acc, extra, outs)


def _mm(x, w, epilogue, *, tm, tn, n_col_blocks, w_col0=0, extras=(), out_dtypes):
    m, k = x.shape
    nj, ni = n_col_blocks, m // tm
    in_specs = [pl.BlockSpec((tm, k), lambda j, i: (i, 0)),
                pl.BlockSpec((k, tn), lambda j, i: (0, w_col0 + j))]
    args = [x, w]
    for arr, kind in extras:
        if kind == "tile":
            spec = pl.BlockSpec((tm, tn), lambda j, i: (i, j))
        elif kind == "tile2":
            spec = pl.BlockSpec((tm, tn), lambda j, i: (i, nj + j))
        elif kind == "row":
            spec = pl.BlockSpec((tm, arr.shape[1]), lambda j, i: (i, 0))
        elif kind == "col":
            spec = pl.BlockSpec((1, tn), lambda j, i: (0, j))
        elif kind == "colmat":
            spec = pl.BlockSpec((arr.shape[0], tn), lambda j, i: (0, j))
        elif kind == "full":
            spec = pl.BlockSpec(arr.shape, lambda j, i, nd=arr.ndim: (0,) * nd)
        else:
            raise ValueError(kind)
        in_specs.append(spec)
        args.append(arr)
    out_specs = [pl.BlockSpec((tm, tn), lambda j, i: (i, j)) for _ in out_dtypes]
    out_shape = [jax.ShapeDtypeStruct((m, nj * tn), dt) for dt in out_dtypes]
    res = pl.pallas_call(
        functools.partial(_mm_kernel, epilogue=epilogue, n_extra=len(extras)),
        grid=(nj, ni),
        in_specs=in_specs,
        out_specs=out_specs,
        out_shape=out_shape,
        compiler_params=_params(2),
    )(*args)
    return res


def _ep_qk(acc, extra, outs, *, dk):
    g_ref, scale_ref, cos_ref, sin_ref, blk_ref = extra
    o32, o16 = outs
    cos = cos_ref[...]
    sin = sin_ref[...]
    blk = blk_ref[...]
    half = dk // 2
    lane = lax.broadcasted_iota(jnp.int32, (acc.shape[0], LANES), 1)
    upper = (lane & half) != 0
    for cb in range(acc.shape[1] // LANES):
        sl = slice(cb * LANES, (cb + 1) * LANES)
        z = acc[:, sl]
        ss = jnp.dot((z * z).astype(BF16), blk, preferred_element_type=F32)
        y = z * lax.rsqrt(ss * (1.0 / dk) + EPS) * g_ref[:, sl]
        partner = jnp.where(upper, pltpu.roll(y, half, 1), pltpu.roll(y, LANES - half, 1))
        r = y * cos + partner * sin
        o32[:, sl] = r
        o16[:, sl] = (r * scale_ref[:, sl]).astype(BF16)


def _ep_copy2(acc, extra, outs):
    outs[0][...] = acc
    outs[1][...] = acc.astype(BF16)


def _ep_gelu(acc, extra, outs):
    outs[0][...] = _gelu(acc).astype(outs[0].dtype)


def _ep_gelu_ln(acc, extra, outs):
    g_ref, b_ref = extra
    y = _gelu(acc)
    yc = y - jnp.mean(y, axis=-1, keepdims=True)
    z = yc * lax.rsqrt(jnp.mean(yc * yc, axis=-1, keepdims=True) + EPS)
    outs[0][...] = z * g_ref[...] + b_ref[...]


def _ep_sigmoid(acc, extra, outs):
    outs[0][...] = _sigmoid(acc).astype(outs[0].dtype)


def _ep_residual(acc, extra, outs):
    outs[0][...] = extra[0][...] + acc


def _ep_ple(acc, extra, outs):
    h_ref, pe_ref, wp_ref = extra
    proj = jnp.dot(pe_ref[...], wp_ref[...], preferred_element_type=F32)
    outs[0][...] = h_ref[...] + _sigmoid(acc) * proj


def _diff_lambda(lam_ref, lam_init):
    v = lam_ref[...]
    a = jnp.sum(v[0:1, :] * v[1:2, :], axis=-1, keepdims=True)
    b = jnp.sum(v[2:3, :] * v[3:4, :], axis=-1, keepdims=True)
    return jnp.exp(a) - jnp.exp(b) + lam_init


def _subln(o, g, lam_init):
    y = o * lax.rsqrt(jnp.mean(o * o, axis=-1, keepdims=True) + EPS) * g
    return y * (1.0 - lam_init)


def _prompt_attn_kernel(q_ref, k_ref, v_ref, lam_ref, g_ref, o_ref,
                        m_sc, l_sc, acc_sc, *, tq, tk, rg, dk, lam_init):
    qi = pl.program_id(2)
    q0 = qi * tq
    q = q_ref[...]
    lane = lax.broadcasted_iota(jnp.int32, q.shape, 1)
    zero = jnp.zeros_like(q)
    q2 = jnp.concatenate([jnp.where(lane < dk, q, zero),
                          jnp.where(lane >= dk, q, zero)], axis=0)
    m_sc[...] = jnp.full(m_sc.shape, NEG_INF, F32)
    l_sc[...] = jnp.zeros(l_sc.shape, F32)
    acc_sc[...] = jnp.zeros(acc_sc.shape, F32)

    def step(j, masked):
        start = pl.multiple_of(j * tk, tk)
        kj = k_ref[pl.ds(start, tk), :]
        vj = v_ref[pl.ds(start, tk), :]
        m_all, l_all, acc_all = m_sc[...], l_sc[...], acc_sc[...]
        m_out, l_out, acc_out = [], [], []
        n_groups = 2 * tq // rg

        def scores(r):
            return lax.dot_general(q2[r * rg:(r + 1) * rg], kj, (((1,), (1,)), ((), ())),
                                   preferred_element_type=F32)

        pending = [scores(r) for r in range(min(AHEAD, n_groups))]
        for r in range(n_groups):
            rows = slice(r * rg, (r + 1) * rg)
            s = pending.pop(0)
            if r + AHEAD < n_groups:
                pending.append(scores(r + AHEAD))
            if masked:
                row = lax.broadcasted_iota(jnp.int32, (rg, tk), 0) + (r * rg) % tq
                col = lax.broadcasted_iota(jnp.int32, (rg, tk), 1)
                s = jnp.where(col - row <= q0 - start, s, NEG_INF)
            m_old = m_all[rows]
            m_new = jnp.maximum(m_old, jnp.max(s, axis=-1, keepdims=True))
            alpha = jnp.exp(m_old - m_new)
            p = jnp.exp(s - m_new)
            l_out.append(alpha * l_all[rows] + jnp.sum(p, axis=-1, keepdims=True))
            acc_out.append(alpha * acc_all[rows] + jnp.dot(
                p.astype(BF16), vj, preferred_element_type=F32))
            m_out.append(m_new)
        m_sc[...] = jnp.concatenate(m_out, axis=0)
        l_sc[...] = jnp.concatenate(l_out, axis=0)
        acc_sc[...] = jnp.concatenate(acc_out, axis=0)

    def body(j, carry):
        step(j, False)
        return carry

    n_full = q0 // tk
    lax.fori_loop(0, n_full, body, 0)
    step(n_full, True)

    lam = _diff_lambda(lam_ref, lam_init)
    o = acc_sc[...] / l_sc[...]
    od = o[:tq] - lam * o[tq:]
    o_ref[...] = _subln(od, g_ref[...], lam_init).astype(o_ref.dtype)


def _prompt_attention(qk16, v16, lam_vec, g_subln, *, b, s, h, dk, lam_init):
    dv = 2 * dk
    assert dv == LANES
    tq = _pick_tile(s, (256, 128))
    tk = _pick_tile(s, (1024, 512, 256, 128))
    assert tk % tq == 0
    nq = s // tq
    return pl.pallas_call(
        functools.partial(_prompt_attn_kernel, tq=tq, tk=tk, rg=min(256, tq), dk=dk,
                          lam_init=lam_init),
        grid=(b, h, nq),
        in_specs=[pl.BlockSpec((tq, LANES), lambda bi, hi, qi: (bi * nq + qi, hi)),
                  pl.BlockSpec((s, LANES), lambda bi, hi, qi: (bi, h + hi)),
                  pl.BlockSpec((s, LANES), lambda bi, hi, qi: (bi, hi)),
                  pl.BlockSpec((SUBLANES, LANES), lambda bi, hi, qi: (0, 0)),
                  pl.BlockSpec((1, LANES), lambda bi, hi, qi: (0, 0))],
        out_specs=pl.BlockSpec((tq, LANES), lambda bi, hi, qi: (bi * nq + qi, hi)),
        out_shape=jax.ShapeDtypeStruct((b * s, h * dv), BF16),
        scratch_shapes=[pltpu.VMEM((2 * tq, 1), F32),
                        pltpu.VMEM((2 * tq, 1), F32),
                        pltpu.VMEM((2 * tq, LANES), F32)],
        compiler_params=_params(3),
    )(qk16, qk16, v16, lam_vec, g_subln.reshape(1, dv))


def _sample_attn_kernel(pt_ref, qbd_ref, knew_ref, vnew_ref, lam_ref, g_ref, *rest,
                        pp, page, n_heads, t_new, lam_init):
    kpages = rest[:pp]
    vpages = rest[pp:2 * pp]
    o_ref = rest[2 * pp]
    kb_sc, vb_sc, m_sc, l_sc, acc_sc = rest[2 * pp + 1:]
    step = pl.program_id(1)
    n_steps = pl.num_programs(1)
    dv = LANES
    rph = 2 * t_new

    @pl.when(step == 0)
    def _():
        m_sc[...] = jnp.full(m_sc.shape, NEG_INF, F32)
        l_sc[...] = jnp.zeros(l_sc.shape, F32)
        acc_sc[...] = jnp.zeros(acc_sc.shape, F32)

    qbd = qbd_ref[...]

    def update(s, v_of_head):
        m_old = m_sc[...]
        m_new = jnp.maximum(m_old, jnp.max(s, axis=-1, keepdims=True))
        alpha = jnp.exp(m_old - m_new)
        p = jnp.exp(s - m_new)
        l_sc[...] = alpha * l_sc[...] + jnp.sum(p, axis=-1, keepdims=True)
        pb = p.astype(BF16)
        for hh in range(n_heads):
            rows = pl.ds(hh * rph, rph)
            acc_sc[rows, :] = alpha[hh * rph:(hh + 1) * rph] * acc_sc[rows, :] + jnp.dot(
                pb[hh * rph:(hh + 1) * rph], v_of_head(hh), preferred_element_type=F32)
        m_sc[...] = m_new

    for p_i in range(pp):
        kb_sc[:, pl.ds(p_i * page, page)] = kpages[p_i][...].astype(BF16)
        for hh in range(n_heads):
            vb_sc[hh, pl.ds(p_i * page, page), :] = (
                vpages[p_i][pl.ds(hh, page, stride=n_heads), :].astype(BF16))
    s = jnp.dot(qbd, kb_sc[...], preferred_element_type=F32)
    update(s, lambda hh: vb_sc[hh])

    @pl.when(step == n_steps - 1)
    def _():
        sn = jnp.dot(qbd, knew_ref[...].astype(BF16), preferred_element_type=F32)
        row = lax.broadcasted_iota(jnp.int32, sn.shape, 0)
        col = lax.broadcasted_iota(jnp.int32, sn.shape, 1)
        keep = col <= (row & (t_new - 1))
        update(jnp.where(keep, sn, NEG_INF), lambda hh: vnew_ref[hh].astype(BF16))

        lam = _diff_lambda(lam_ref, lam_init)
        o = acc_sc[...] / l_sc[...]
        for hh in range(n_heads):
            r1 = hh * rph
            od = o[r1:r1 + t_new] - lam * o[r1 + t_new:r1 + rph]
            o_ref[:, hh * dv:(hh + 1) * dv] = _subln(od, g_ref[...], lam_init)


def _sample_attention(qbd, k_new_t, v_new, cache_kt, cache_v2, layer, page_table, lam_vec,
                      g_subln, *, n_heads, t_new, lam_init):
    db, rows, width = qbd.shape
    page = cache_kt.shape[3]
    n_pages = page_table.shape[1]
    pp = _pick_tile(n_pages, (8, 4, 2, 1))
    n_steps = n_pages // pp
    dv = LANES
    assert page == LANES and t_new & (t_new - 1) == 0 and t_new % SUBLANES == 0

    def kspec(p_i):
        return pl.BlockSpec((None, None, width, page),
                            lambda b, st, pt: (layer, pt[b, st * pp + p_i], 0, 0))

    def vspec(p_i):
        return pl.BlockSpec((None, None, page * n_heads, dv),
                            lambda b, st, pt: (layer, pt[b, st * pp + p_i], 0, 0))

    in_specs = [pl.BlockSpec((None, rows, width), lambda b, st, pt: (b, 0, 0)),
                pl.BlockSpec((None, width, LANES), lambda b, st, pt: (b, 0, 0)),
                pl.BlockSpec((None, n_heads, LANES, dv), lambda b, st, pt: (b, 0, 0, 0)),
                pl.BlockSpec((SUBLANES, LANES), lambda b, st, pt: (0, 0)),
                pl.BlockSpec((1, dv), lambda b, st, pt: (0, 0))]
    in_specs += [kspec(p_i) for p_i in range(pp)]
    in_specs += [vspec(p_i) for p_i in range(pp)]
    grid_spec = pltpu.PrefetchScalarGridSpec(
        num_scalar_prefetch=1,
        grid=(db, n_steps),
        in_specs=in_specs,
        out_specs=pl.BlockSpec((t_new, n_heads * dv), lambda b, st, pt: (b, 0)),
        scratch_shapes=[pltpu.VMEM((width, pp * page), BF16),
                        pltpu.VMEM((n_heads, pp * page, dv), BF16),
                        pltpu.VMEM((rows, 1), F32),
                        pltpu.VMEM((rows, 1), F32),
                        pltpu.VMEM((rows, dv), F32)])
    return pl.pallas_call(
        functools.partial(_sample_attn_kernel, pp=pp, page=page, n_heads=n_heads,
                          t_new=t_new, lam_init=lam_init),
        grid_spec=grid_spec,
        out_shape=jax.ShapeDtypeStruct((db * t_new, n_heads * dv), F32),
        compiler_params=_params(2),
    )(page_table, qbd, k_new_t, v_new, lam_vec, g_subln.reshape(1, dv),
      *([cache_kt] * pp), *([cache_v2] * pp))


def _gmlp_kernel(u_ref, gv_ref, w_ref, b_ref, o_ref, *, groups):
    for g in range(groups):
        sl = slice(g * LANES, (g + 1) * LANES)
        s = jnp.dot(w_ref[g], gv_ref[:, sl].astype(BF16), preferred_element_type=F32)
        s = s + b_ref[g]
        o_ref[:, sl] = (u_ref[:, sl].astype(F32) * s).astype(o_ref.dtype)


def _gmlp(u16, gv, wmix, bmix, *, n_prompt_chunks):
    t, width = gv.shape
    groups, chunk = wmix.shape[1], wmix.shape[2]
    assert chunk == LANES and width == groups * LANES
    sel = lambda c: (jnp.where(c < n_prompt_chunks, 0, 1), 0, 0, 0)
    return pl.pallas_call(
        functools.partial(_gmlp_kernel, groups=groups),
        grid=(t // chunk,),
        in_specs=[pl.BlockSpec((chunk, width), lambda c: (c, 0)),
                  pl.BlockSpec((chunk, width), lambda c: (c, 0)),
                  pl.BlockSpec((None, groups, chunk, chunk), sel),
                  pl.BlockSpec((None, groups, chunk, LANES), sel)],
        out_specs=pl.BlockSpec((chunk, width), lambda c: (c, 0)),
        out_shape=jax.ShapeDtypeStruct((t, width), BF16),
        compiler_params=_params(1),
    )(u16, gv, wmix, bmix)


def _merge_kernel(att_ref, gm_ref, wa_ref, wg_ref, ga_ref, gg_ref, o_ref):
    a = jnp.dot(att_ref[...], wa_ref[...], preferred_element_type=F32)
    g = jnp.dot(gm_ref[...], wg_ref[...], preferred_element_type=F32)
    o_ref[...] = (ga_ref[...].astype(F32) * a + gg_ref[...].astype(F32) * g).astype(o_ref.dtype)


def _merge(att16, gm16, wa, wg, gates16, *, tm, tn):
    t, ka = att16.shape
    kg = gm16.shape[1]
    d = wa.shape[1]
    nj = d // tn
    return pl.pallas_call(
        _merge_kernel,
        grid=(nj, t // tm),
        in_specs=[pl.BlockSpec((tm, ka), lambda j, i: (i, 0)),
                  pl.BlockSpec((tm, kg), lambda j, i: (i, 0)),
                  pl.BlockSpec((ka, tn), lambda j, i: (0, j)),
                  pl.BlockSpec((kg, tn), lambda j, i: (0, j)),
                  pl.BlockSpec((tm, tn), lambda j, i: (i, j)),
                  pl.BlockSpec((tm, tn), lambda j, i: (i, nj + j))],
        out_specs=pl.BlockSpec((tm, tn), lambda j, i: (i, j)),
        out_shape=jax.ShapeDtypeStruct((t, d), BF16),
        compiler_params=_params(2),
    )(att16, gm16, wa, wg, gates16, gates16)


def _router_kernel(h_ref, g_ref, wh_ref, wl_ref, b_ref, xn_ref, ti_ref, tw_ref,
                   *, n_experts):
    x = h_ref[...]
    xn = x * lax.rsqrt(jnp.mean(x * x, axis=-1, keepdims=True) + EPS) * g_ref[...]
    tm, d = xn.shape
    xh = xn.astype(BF16)
    xl = (xn - xh.astype(F32)).astype(BF16)
    wh = wh_ref[...]
    logits = (jnp.dot(xh, wh, preferred_element_type=F32)
              + jnp.dot(xl, wh, preferred_element_type=F32)
              + jnp.dot(xh, wl_ref[...], preferred_element_type=F32)) + b_ref[...]
    lane = lax.broadcasted_iota(jnp.int32, logits.shape, 1)
    lane_f = lane.astype(F32)
    cur = jnp.where(lane < n_experts, logits, MASKED)
    vals, idxs = [], []
    for _ in range(TOP_K):
        m = jnp.max(cur, axis=-1, keepdims=True)
        idx = jnp.min(jnp.where(cur == m, lane_f, float(LANES)), axis=-1, keepdims=True)
        vals.append(m)
        idxs.append(idx)
        cur = jnp.where(lane_f == idx, MASKED, cur)
    es = [jnp.exp(v - vals[0]) for v in vals]
    tot = es[0]
    for e in es[1:]:
        tot = tot + e
    ti = jnp.zeros(logits.shape, F32)
    tw = jnp.zeros(logits.shape, F32)
    for kk in range(TOP_K):
        ti = jnp.where(lane == kk, idxs[kk], ti)
        tw = jnp.where(lane == kk, es[kk] / tot, tw)
    ti_ref[...] = ti.astype(jnp.int32)
    tw_ref[...] = tw
    spt = d // LANES
    for s in range(spt):
        xn_ref[pl.ds(s, tm, stride=spt), :] = xn[:, s * LANES:(s + 1) * LANES]


def _router(h, g_ffn, wr_hi, wr_lo, b_pad, *, n_experts, tm):
    t, d = h.shape
    spt = d // LANES
    return pl.pallas_call(
        functools.partial(_router_kernel, n_experts=n_experts),
        grid=(t // tm,),
        in_specs=[pl.BlockSpec((tm, d), lambda i: (i, 0)),
                  pl.BlockSpec((1, d), lambda i: (0, 0)),
                  pl.BlockSpec((d, LANES), lambda i: (0, 0)),
                  pl.BlockSpec((d, LANES), lambda i: (0, 0)),
                  pl.BlockSpec((1, LANES), lambda i: (0, 0))],
        out_specs=[pl.BlockSpec((tm * spt, LANES), lambda i: (i, 0)),
                   pl.BlockSpec((tm, LANES), lambda i: (i, 0)),
                   pl.BlockSpec((tm, LANES), lambda i: (i, 0))],
        out_shape=[jax.ShapeDtypeStruct((t * spt, LANES), F32),
                   jax.ShapeDtypeStruct((t, LANES), jnp.int32),
                   jax.ShapeDtypeStruct((t, LANES), F32)],
        compiler_params=_params(1),
    )(h, g_ffn.reshape(1, d), wr_hi, wr_lo, b_pad)


def _gather_kernel(idx_ref, nv_ref, src_ref, o_ref, buf, sem, *, tm, spt):
    t = pl.program_id(0)
    n_live = nv_ref[0]

    def copy(tile, r, slot):
        tok = idx_ref[tile * tm + r]
        return pltpu.make_async_copy(
            src_ref.at[pl.ds(pl.multiple_of(tok * spt, spt), spt), :],
            buf.at[slot, pl.ds(pl.multiple_of(r * spt, spt), spt), :],
            sem.at[slot])

    def start_tile(tile, slot):
        def body(i, c):
            for u in range(DMA_UNROLL):
                copy(tile, i * DMA_UNROLL + u, slot).start(priority=u % 2)
            return c
        lax.fori_loop(0, tm // DMA_UNROLL, body, 0)

    def wait_tile(tile, slot):
        def body(r, c):
            copy(tile, r, slot).wait()
            return c
        lax.fori_loop(0, tm, body, 0, unroll=DMA_UNROLL)

    @pl.when(jnp.logical_and(t == 0, n_live > 0))
    def _():
        start_tile(0, 0)

    for slot in (0, 1):
        mine = (t % 2) == slot

        @pl.when(jnp.logical_and(mine, t + 1 < n_live))
        def _():
            start_tile(t + 1, 1 - slot)

        @pl.when(jnp.logical_and(mine, t < n_live))
        def _():
            wait_tile(t, slot)
            for s in range(spt):
                o_ref[:, s * LANES:(s + 1) * LANES] = (
                    buf[slot, pl.ds(s, tm, stride=spt), :].astype(o_ref.dtype))

    @pl.when(t >= n_live)
    def _():
        o_ref[...] = jnp.zeros(o_ref.shape, o_ref.dtype)


def _gather_rows(src, idx, n_valid_tiles, *, n_tiles, tm, spt):
    grid_spec = pltpu.PrefetchScalarGridSpec(
        num_scalar_prefetch=2,
        grid=(n_tiles,),
        in_specs=[pl.BlockSpec(memory_space=pl.ANY)],
        out_specs=pl.BlockSpec((tm, spt * LANES), lambda t, idx, nv: (t, 0)),
        scratch_shapes=[pltpu.VMEM((2, tm * spt, LANES), src.dtype),
                        pltpu.SemaphoreType.DMA((2,))])
    return pl.pallas_call(
        functools.partial(_gather_kernel, tm=tm, spt=spt),
        grid_spec=grid_spec,
        out_shape=jax.ShapeDtypeStruct((n_tiles * tm, spt * LANES), BF16),
        compiler_params=_params(1),
    )(idx, n_valid_tiles, src)


def _new_group(te_ref, t):
    prev = te_ref[jnp.maximum(t - 1, 0)]
    return jnp.logical_or(t == 0, te_ref[t] != prev)


def _moe_up_kernel(te_ref, nv_ref, xs_ref, wg_ref, wl_ref, bg_ref, bl_ref, o_ref,
                   wg_sc, wl_sc):
    t = pl.program_id(1)
    valid = t < nv_ref[0]

    @pl.when(jnp.logical_and(valid, _new_group(te_ref, t)))
    def _():
        wg_sc[...] = wg_ref[...].astype(BF16)
        wl_sc[...] = wl_ref[...].astype(BF16)

    @pl.when(valid)
    def _():
        x = xs_ref[...]
        gl =jnp.dot(x, wg_sc[...], preferred_element_type=F32) + bg_ref[...]
        ln = jnp.dot(x, wl_sc[...], preferred_element_type=F32) + bl_ref[...]
        glu = jnp.minimum(gl, SWIGLU_LIMIT)
        lin = jnp.clip(ln, -SWIGLU_LIMIT, SWIGLU_LIMIT)
        act = glu * _sigmoid(SWIGLU_ALPHA * glu) * (lin + 1.0)
        o_ref[...] = act.astype(o_ref.dtype)

    @pl.when(jnp.logical_not(valid))
    def _():
        o_ref[...] = jnp.zeros(o_ref.shape, o_ref.dtype)


def _moe_up(xs, w_gate_up, b_gate_up, tile_expert, n_valid, *, n_tiles, tm, fc):
    e, d, f2 = w_gate_up.shape
    f = f2 // 2
    nc = f // fc
    last = lambda t, nv: jnp.minimum(t, nv[0] - 1)
    grid_spec = pltpu.PrefetchScalarGridSpec(
        num_scalar_prefetch=2,
        grid=(nc, n_tiles),
        in_specs=[pl.BlockSpec((tm, d), lambda c, t, te, nv: (last(t, nv), 0)),
                  pl.BlockSpec((None, d, fc), lambda c, t, te, nv: (te[t], 0, c)),
                  pl.BlockSpec((None, d, fc), lambda c, t, te, nv: (te[t], 0, nc + c)),
                  pl.BlockSpec((None, 1, fc), lambda c, t, te, nv: (te[t], 0, c)),
                  pl.BlockSpec((None, 1, fc), lambda c, t, te, nv: (te[t], 0, nc + c))],
        out_specs=pl.BlockSpec((tm, fc), lambda c, t, te, nv: (t, c)),
        scratch_shapes=[pltpu.VMEM((d, fc), BF16), pltpu.VMEM((d, fc), BF16)])
    return pl.pallas_call(
        _moe_up_kernel,
        grid_spec=grid_spec,
        out_shape=jax.ShapeDtypeStruct((n_tiles * tm, f), BF16),
        compiler_params=_params(2),
    )(tile_expert, n_valid, xs, w_gate_up, w_gate_up,
      b_gate_up.reshape(e, 1, f2), b_gate_up.reshape(e, 1, f2))


def _moe_down_kernel(te_ref, nv_ref, a_ref, w_ref, b_ref, o_ref, w_sc, *, tm, spo):
    t = pl.program_id(1)
    valid = t < nv_ref[0]

    @pl.when(jnp.logical_and(valid, _new_group(te_ref, t)))
    def _():
        w_sc[...] = w_ref[...].astype(BF16)

    @pl.when(valid)
    def _():
        y = jnp.dot(a_ref[...], w_sc[...], preferred_element_type=F32) + b_ref[...]
        for s in range(y.shape[1] // LANES):
            o_ref[pl.ds(s, tm, stride=spo), :] = y[:, s * LANES:(s + 1) * LANES]

    @pl.when(jnp.logical_not(valid))
    def _():
        o_ref[...] = jnp.zeros(o_ref.shape, o_ref.dtype)


def _moe_down(act, w_down, b_down, tile_expert, n_valid, *, n_tiles, tm, tn):
    e, f, d = w_down.shape
    nn = d // tn
    spo = tn // LANES
    last = lambda t, nv: jnp.minimum(t, nv[0] - 1)
    grid_spec = pltpu.PrefetchScalarGridSpec(
        num_scalar_prefetch=2,
        grid=(nn, n_tiles),
        in_specs=[pl.BlockSpec((tm, f), lambda n, t, te, nv: (last(t, nv), 0)),
                  pl.BlockSpec((None, f, tn), lambda n, t, te, nv: (te[t], 0, n)),
                  pl.BlockSpec((None, 1, tn), lambda n, t, te, nv: (te[t], 0, n))],
        out_specs=pl.BlockSpec((None, tm * spo, LANES), lambda n, t, te, nv: (n, t, 0)),
        scratch_shapes=[pltpu.VMEM((f, tn), BF16)])
    return pl.pallas_call(
        functools.partial(_moe_down_kernel, tm=tm, spo=spo),
        grid_spec=grid_spec,
        out_shape=jax.ShapeDtypeStruct((nn, n_tiles * tm * spo, LANES), F32),
        compiler_params=_params(2),
    )(tile_expert, n_valid, act, w_down, b_down.reshape(e, 1, d))


def _combine_kernel(pos_ref, ys_ref, h_ref, tw_ref, g_ref, o_ref, on_ref, buf, sem,
                    *, tc, nn, spo):
    i = pl.program_id(0)
    n_steps = pl.num_programs(0)

    def copy(tile, r, kk, n, slot):
        p = pos_ref[(tile * tc + r) * TOP_K + kk]
        return pltpu.make_async_copy(
            ys_ref.at[n, pl.ds(pl.multiple_of(p * spo, spo), spo), :],
            buf.at[slot, kk * nn + n, pl.ds(pl.multiple_of(r * spo, spo), spo), :],
            sem.at[slot])

    def start_tile(tile, slot):
        def body(r, c):
            for kk in range(TOP_K):
                for n in range(nn):
                    copy(tile, r, kk, n, slot).start(priority=kk % 2)
            return c
        lax.fori_loop(0, tc, body, 0, unroll=DMA_UNROLL // 2)

    def wait_tile(tile, slot):
        def body(r, c):
            for kk in range(TOP_K):
                for n in range(nn):
                    copy(tile, r, kk, n, slot).wait()
            return c
        lax.fori_loop(0, tc, body, 0, unroll=DMA_UNROLL // 2)

    @pl.when(i == 0)
    def _():
        start_tile(0, 0)

    for slot in (0, 1):
        mine = (i % 2) == slot

        @pl.when(jnp.logical_and(mine, i + 1 < n_steps))
        def _():
            start_tile(i + 1, 1 - slot)

        @pl.when(mine)
        def _():
            wait_tile(i, slot)
            tw = tw_ref[...]
            cols = []
            for n in range(nn):
                for s in range(spo):
                    acc = None
                    for kk in range(TOP_K):
                        piece = (buf[slot, kk * nn + n, pl.ds(s, tc, stride=spo), :]
                                 * tw[:, kk:kk + 1])
                        acc = piece if acc is None else acc + piece
                    cols.append(acc)
            y = jnp.concatenate(cols, axis=1)
            h2 = h_ref[...] + y
            o_ref[...] = h2
            hn = h2 * lax.rsqrt(jnp.mean(h2 * h2, axis=-1, keepdims=True) + EPS) * g_ref[...]
            on_ref[...] = hn.astype(on_ref.dtype)


def _combine(ys, pos, h, top_w, g_ple, *, tc):
    t, d = h.shape
    nn, _, _ = ys.shape
    spo = d // nn // LANES
    grid_spec = pltpu.PrefetchScalarGridSpec(
        num_scalar_prefetch=1,
        grid=(t // tc,),
        in_specs=[pl.BlockSpec(memory_space=pl.ANY),
                  pl.BlockSpec((tc, d), lambda i, pos: (i, 0)),
                  pl.BlockSpec((tc, LANES), lambda i, pos: (i, 0)),
                  pl.BlockSpec((1, d), lambda i, pos: (0, 0))],
        out_specs=[pl.BlockSpec((tc, d), lambda i, pos: (i, 0)),
                   pl.BlockSpec((tc, d), lambda i, pos: (i, 0))],
        scratch_shapes=[pltpu.VMEM((2, TOP_K * nn, tc * spo, LANES), F32),
                        pltpu.SemaphoreType.DMA((2,))])
    return pl.pallas_call(
        functools.partial(_combine_kernel, tc=tc, nn=nn, spo=spo),
        grid_spec=grid_spec,
        out_shape=[jax.ShapeDtypeStruct((t, d), F32),
                   jax.ShapeDtypeStruct((t, d), BF16)],
        compiler_params=_params(1),
    )(pos, ys, h, top_w, g_ple.reshape(1, d))


def _routing_plan(top_i, n_experts, tm, n_tiles):
    eid = top_i[:, :TOP_K].reshape(-1)
    a = eid.shape[0]
    onehot = (eid[:, None] == jnp.arange(n_experts, dtype=jnp.int32)[None, :]).astype(jnp.int32)
    csum = jnp.cumsum(onehot, axis=0)
    counts = csum[-1]
    rank = jnp.sum(onehot * csum, axis=1) - 1
    tiles_per = (counts + tm - 1) // tm
    cum_tiles = jnp.cumsum(tiles_per)
    pad_off = (cum_tiles - tiles_per) * tm
    pos = jnp.sum(onehot * pad_off[None, :], axis=1) + rank
    row_token = jnp.zeros((n_tiles * tm,), jnp.int32).at[pos].set(
        jnp.arange(a, dtype=jnp.int32) // TOP_K)
    n_valid = cum_tiles[-1]
    tile_ids = jnp.arange(n_tiles, dtype=jnp.int32)
    te = jnp.sum((tile_ids[:, None] >= cum_tiles[None, :]).astype(jnp.int32), axis=1)
    last_live = jnp.sum((n_valid - 1 >= cum_tiles).astype(jnp.int32))
    te = jnp.where(tile_ids < n_valid, te, last_live)
    te = jnp.minimum(te, n_experts - 1)
    return row_token, pos.astype(jnp.int32), te.astype(jnp.int32), n_valid.reshape(1).astype(jnp.int32)


def _rope_tables(pos, dk):
    half = dk // 2
    inv = jnp.power(ROPE_THETA, -jnp.arange(half, dtype=F32) * 2.0 / dk)
    ang = pos.astype(F32)[:, None] * inv[None, :]
    cos = jnp.cos(ang)
    sin = jnp.sin(ang)
    reps = LANES // dk
    cos_t = jnp.tile(jnp.concatenate([cos, cos], axis=1), (1, reps))
    sin_t = jnp.tile(jnp.concatenate([-sin, sin], axis=1), (1, reps))
    return cos_t, sin_t


def _layer(x, pe16, p, cache_k, cache_v, layer, page_table, dims, lam_init):
    b, s, db, t_new, n_heads, dk = dims
    t, d = x.shape
    tp = b * s
    dv = 2 * dk
    qk_w = 2 * n_heads * dk
    att_w = n_heads * dv
    gm_w = p["w_gmlp_out"].shape[0]
    n_experts = p["w_router"].shape[1]
    f = p["w_down"].shape[1]
    assert qk_w == att_w == gm_w, "segment widths must match the column block"
    tn = qk_w
    tm = _pick_tile(t, (768, 512, 384, 256, 128))

    w_in = p["w_in"].astype(BF16)
    xn = _rmsnorm(x, p["g_mix"], tm)

    g_qk = jnp.concatenate([jnp.tile(p["g_q"], qk_w // dk), jnp.tile(p["g_k"], qk_w // dk)]
                           ).reshape(1, 2 * qk_w)
    scale = jnp.concatenate([jnp.full((qk_w,), dk ** -0.5, F32), jnp.ones((qk_w,), F32)]
                            ).reshape(1, 2 * qk_w)
    lane = jnp.arange(LANES)
    blk = (lane[:, None] // dk == lane[None, :] // dk).astype(BF16)
    qk32, qk16 = _mm(xn, w_in, functools.partial(_ep_qk, dk=dk), tm=tm, tn=tn,
                     n_col_blocks=2, w_col0=0,
                     extras=[(g_qk, "col"), (scale, "col"), (p["cos"], "row"),
                             (p["sin"], "row"), (blk, "full")],
                     out_dtypes=[F32, BF16])
    v32, v16 = _mm(xn, w_in, _ep_copy2, tm=tm, tn=tn, n_col_blocks=1, w_col0=2,
                   out_dtypes=[F32, BF16])
    (u16,) = _mm(xn, w_in, _ep_gelu, tm=tm, tn=tn, n_col_blocks=1, w_col0=3,
                 out_dtypes=[BF16])
    (gv,) = _mm(xn, w_in, _ep_gelu_ln, tm=tm, tn=tn, n_col_blocks=1, w_col0=4,
                extras=[(p["g_ln_v"].reshape(1, gm_w), "col"),
                        (p["b_ln_v"].reshape(1, gm_w), "col")],
                out_dtypes=[F32])
    (gates16,) = _mm(xn, w_in, _ep_sigmoid, tm=tm, tn=tn, n_col_blocks=2 * d // tn, w_col0=5,
                     out_dtypes=[BF16])

    lam_vec = jnp.zeros((SUBLANES, LANES), F32).at[:4, :dk].set(
        jnp.stack([p["lam_q1"], p["lam_k1"], p["lam_q2"], p["lam_k2"]]))
    att_p = _prompt_attention(qk16, v16, lam_vec, p["g_subln"], b=b, s=s, h=n_heads,
                              dk=dk, lam_init=lam_init)
    n_sub = 2 * n_heads
    q_s = qk16[tp:, :qk_w].reshape(db, t_new, n_sub, dk)
    qbd = jnp.einsum("btgd,gk->bgtkd", q_s, jnp.eye(n_sub, dtype=BF16)
                     ).reshape(db, n_sub * t_new, qk_w)
    k_new = qk32[tp:, qk_w:].reshape(db, t_new, n_sub * dk)
    k_new_t = jnp.pad(jnp.transpose(k_new, (0, 2, 1)), ((0, 0), (0, 0), (0, LANES - t_new)))
    v_new = jnp.pad(jnp.transpose(v32[tp:].reshape(db, t_new, n_heads, dv), (0, 2, 1, 3)),
                    ((0, 0), (0, 0), (0, LANES - t_new), (0, 0)))
    att_s = _sample_attention(qbd, k_new_t, v_new, cache_k, cache_v, layer, page_table, lam_vec,
                              p["g_subln"], n_heads=n_heads, t_new=t_new, lam_init=lam_init)
    att16 = jnp.concatenate([att_p, att_s.astype(BF16)], axis=0)

    gm16 = _gmlp(u16, gv, p["wmix"], p["bmix"], n_prompt_chunks=tp // LANES)

    merged = _merge(att16, gm16, p["w_att_out"].astype(BF16), p["w_gmlp_out"].astype(BF16),
                    gates16, tm=tm, tn=tn)
    (h,) = _mm(merged, p["w_o"].astype(BF16), _ep_residual, tm=tm, tn=tn,
               n_col_blocks=d // tn, extras=[(x, "tile")], out_dtypes=[F32])

    wr = jnp.zeros((d, LANES), F32).at[:, :n_experts].set(p["w_router"])
    wr_hi = wr.astype(BF16)
    wr_lo = (wr - wr_hi.astype(F32)).astype(BF16)
    b_pad = jnp.zeros((1, LANES), F32).at[0, :n_experts].set(p["b_router"])
    tmr = _pick_tile(t, (256, 128))
    hn_slabs, top_i, top_w = _router(h, p["g_ffn"], wr_hi, wr_lo, b_pad, n_experts=n_experts,
                                     tm=tmr)
    tme = 256 if (t * TOP_K) % 256 == 0 else 128
    n_tiles = (t * TOP_K) // tme + n_experts
    row_token, pos, te, n_valid = _routing_plan(top_i, n_experts, tme, n_tiles)
    xs = _gather_rows(hn_slabs, row_token, n_valid, n_tiles=n_tiles, tm=tme, spt=d // LANES)
    act = _moe_up(xs, p["w_gate_up"], p["b_gate_up"], te, n_valid,
                  n_tiles=n_tiles, tm=tme, fc=min(1024, f))
    ys = _moe_down(act, p["w_down"], p["b_down"], te, n_valid, n_tiles=n_tiles, tm=tme,
                   tn=min(2048, d))
    h2, h2n = _combine(ys, pos, h, top_w, p["g_ple"], tc=LANES)

    (out,) = _mm(h2n, p["w_ple_gate"].astype(BF16), _ep_ple, tm=tm, tn=tn, n_col_blocks=d // tn,
                 extras=[(h2, "tile"), (pe16, "row"), (p["w_ple_proj"].astype(BF16), "colmat")],
                 out_dtypes=[F32])
    k32 = qk32[:, qk_w:]
    return out, k32, v32, gv


def kernel(x_prompt, x_sample, cache_k, cache_v, page_table, p_prompt, p_sample, g_mix, w_in, g_q, g_k, lam_q1, lam_k1, lam_q2, lam_k2, g_subln, g_ln_v, b_ln_v, w_spatial, b_spatial, w_att_out, w_gmlp_out, w_o, g_ffn, w_router, b_router, w_gate_up, b_gate_up, w_down, b_down, g_ple, w_ple_gate, w_ple_proj):
    b, s, d = x_prompt.shape
    db, t_new, _ = x_sample.shape
    depth = w_in.shape[0]
    page = cache_k.shape[2]
    n_heads, dv = cache_v.shape[3], cache_v.shape[4]
    dk = cache_k.shape[4]
    n_pages = page_table.shape[1]
    tp, ts = b * s, db * t_new
    groups, chunk = w_spatial.shape[1], w_spatial.shape[2]
    assert chunk == LANES and dv == LANES and ts % LANES == 0 and s % LANES == 0

    pos_all = jnp.concatenate([jnp.tile(jnp.arange(s), b),
                               jnp.tile(n_pages * page + jnp.arange(t_new), db)])
    cos_t, sin_t = _rope_tables(pos_all, dk)
    x = jnp.concatenate([x_prompt.reshape(tp, d), x_sample.reshape(ts, d)], axis=0)
    tril = jnp.tril(jnp.ones((chunk, chunk), F32))
    seq_per_chunk = chunk // t_new
    n_pool = cache_k.shape[1]
    cache_kt = jnp.transpose(cache_k, (0, 1, 3, 4, 2)).reshape(depth, n_pool, 2 * n_heads * dk, page)
    cache_v2 = cache_v.reshape(depth, n_pool, page * n_heads, dv)

    kp, vp, ks, vs, gs = [], [], [], [], []
    for i in range(depth):
        lam_init = 0.8 - 0.6 * math.exp(-0.3 * i)
        ws = w_spatial[i] * tril
        ws_s = jnp.einsum("ab,gts->gatbs", jnp.eye(seq_per_chunk, dtype=F32),
                          ws[:, :t_new, :t_new]).reshape(groups, chunk, chunk)
        wmix = jnp.stack([ws, ws_s]).astype(BF16)
        b_p = b_spatial[i]
        b_s = jnp.tile(b_spatial[i][:, :t_new], (1, seq_per_chunk))
        bmix = jnp.broadcast_to(jnp.stack([b_p, b_s])[..., None],
                                (2, groups, chunk, LANES)).astype(F32)
        p = dict(g_mix=g_mix[i], w_in=w_in[i], g_q=g_q[i], g_k=g_k[i], g_subln=g_subln[i],
                 g_ln_v=g_ln_v[i], b_ln_v=b_ln_v[i], w_att_out=w_att_out[i],
                 w_gmlp_out=w_gmlp_out[i], w_o=w_o[i], g_ffn=g_ffn[i], w_router=w_router[i],
                 b_router=b_router[i], w_gate_up=w_gate_up[i], b_gate_up=b_gate_up[i],
                 w_down=w_down[i], b_down=b_down[i], g_ple=g_ple[i], w_ple_gate=w_ple_gate[i],
                 w_ple_proj=w_ple_proj[i], lam_q1=lam_q1[i], lam_k1=lam_k1[i],
                 lam_q2=lam_q2[i], lam_k2=lam_k2[i], cos=cos_t, sin=sin_t,
                 wmix=wmix, bmix=bmix)
        pe16 = jnp.concatenate([p_prompt[i].reshape(tp, -1), p_sample[i].reshape(ts, -1)],
                               axis=0).astype(BF16)
        x, k32, v32, gv = _layer(x, pe16, p, cache_kt, cache_v2, i, page_table,
                                 (b, s, db, t_new, n_heads, dk), lam_init)
        kp.append(k32[:tp].reshape(b, s, 2 * n_heads, dk))
        vp.append(v32[:tp].reshape(b, s, n_heads, dv))
        ks.append(k32[tp:].reshape(db, t_new, 2 * n_heads, dk))
        vs.append(v32[tp:].reshape(db, t_new, n_heads, dv))
        gs.append(gv[tp:].reshape(db, t_new, -1))
    return (x[:tp].reshape(b, s, d), x[tp:].reshape(db, t_new, d),
            jnp.stack(kp), jnp.stack(vp), jnp.stack(ks), jnp.stack(vs), jnp.stack(gs))
```
